```python
import jax
import jax.numpy as jnp
from jax import lax
import numpy as np

D_MODEL = 2048
BATCH = 16
SEQ = 2048
DEPTH = 2
DEC_BATCH = 4
DEC_SEQ = 8192
PAST_LEN = 128

HEAD_DIM = 128
MIX_HEADS = D_MODEL // HEAD_DIM
MEM_HEADS = 4
TOK_HEADS = MIX_HEADS - MEM_HEADS
TOK_WIDTH = TOK_HEADS * HEAD_DIM
MEM_WIDTH = MEM_HEADS * HEAD_DIM
MIX_WIDTH = TOK_WIDTH + MEM_WIDTH
N_MEM = 256
D_FF = 4 * D_MODEL
N_MIXERS = 2
N_RET = (DEPTH + 1) // 2
N_NA = DEPTH // 2
RET_CHUNK = 128
RET_DECAY_BASE = 5.0
ROPE_BASE = 10000.0
GRID_W = 64
NA_KH_MAX = 8
NA_KW = 16
NA_COL_BLOCK = 16
NA_COL_SPAN = 32
N_COL_BLOCKS = GRID_W // NA_COL_BLOCK
NORM_EPS = 1e-6

kernel_name = 'hybrid_retention_natten_encoder'


def rms_norm(x, gain):
    xf = x.astype(jnp.float32)
    xf = xf * lax.rsqrt(jnp.mean(xf * xf, axis=-1, keepdims=True) + NORM_EPS)
    return (xf * gain.astype(jnp.float32)).astype(x.dtype)


def rotary(x):
    l, dh = x.shape[1], x.shape[-1]
    half = dh // 2
    inv_freq = ROPE_BASE ** (-jnp.arange(half, dtype=jnp.float32) / half)
    ang = jnp.arange(l, dtype=jnp.float32)[:, None] * inv_freq[None, :]
    cos = jnp.cos(ang)[None, :, None, :]
    sin = jnp.sin(ang)[None, :, None, :]
    xf = x.astype(jnp.float32)
    x1, x2 = xf[..., :half], xf[..., half:]
    return jnp.concatenate([x1 * cos - x2 * sin, x1 * sin + x2 * cos], axis=-1)


def retention_one_direction(q, k, v, log_gamma, strict):
    b, h, n, c, dh = q.shape
    idx = jnp.arange(c, dtype=jnp.float32)
    diff = idx[:, None] - idx[None, :]
    mask = (diff > 0) if strict else (diff >= 0)
    decay_intra = jnp.where(mask[None], jnp.exp(jnp.where(mask, diff, 0.0)[None] * log_gamma[:, None, None]), 0.0)
    scores = jnp.einsum('bhnid,bhnjd->bhnij', q, k) * decay_intra[None, :, None]
    y_intra = jnp.einsum('bhnij,bhnjd->bhnid', scores, v)
    k_decay = jnp.exp((c - 1 - idx)[None, :] * log_gamma[:, None])
    kv = jnp.einsum('bhnjd,bhnje->nbhde', k * k_decay[None, :, None, :, None], v)
    chunk_decay = jnp.exp(c * log_gamma)[None, :, None, None]

    def step(state, kv_n):
        return chunk_decay * state + kv_n, state

    _, states = lax.scan(step, jnp.zeros((b, h, dh, dh), jnp.float32), kv)
    q_decay = jnp.exp((idx + 1.0)[None, :] * log_gamma[:, None])
    y_cross = jnp.einsum('bhnid,nbhde->bhnie', q * q_decay[None, :, None, :, None], states)
    return y_intra + y_cross


def retention_mixer(q, k, v, gate, decay_exp):
    b, l, _ = q.shape
    n = l // RET_CHUNK
    log_gamma = jnp.log1p(-jnp.exp2(-decay_exp.astype(jnp.float32)))
    heads = lambda t: t.reshape(b, l, TOK_HEADS, HEAD_DIM)
    qr = rotary(heads(q))
    kr = rotary(heads(k)) * (HEAD_DIM ** -0.5)
    vf = heads(v).astype(jnp.float32)
    chunks = lambda t: t.reshape(b, n, RET_CHUNK, TOK_HEADS, HEAD_DIM).transpose(0, 3, 1, 2, 4)
    qc, kc, vc = chunks(qr), chunks(kr), chunks(vf)
    flip = lambda t: t[:, :, ::-1, ::-1]
    y = (retention_one_direction(qc, kc, vc, log_gamma[0], False)
         + flip(retention_one_direction(flip(qc), flip(kc), flip(vc), log_gamma[1], True)))
    y = y.transpose(0, 2, 3, 1, 4).reshape(b, l, TOK_HEADS, HEAD_DIM)
    y = y * lax.rsqrt(jnp.mean(y * y, axis=-1, keepdims=True) + NORM_EPS)
    return jax.nn.silu(gate) * y.reshape(b, l, TOK_WIDTH).astype(gate.dtype)


def na_column_tables():
    cb = np.arange(N_COL_BLOCKS)
    q_col = cb[:, None] * NA_COL_BLOCK + np.arange(NA_COL_BLOCK)[None, :]
    win_start = np.clip(q_col - NA_KW // 2, 0, GRID_W - NA_KW)
    blk_start = np.clip(cb * NA_COL_BLOCK - NA_KW // 2, 0, GRID_W - NA_COL_SPAN)
    key_col = blk_start[:, None] + np.arange(NA_COL_SPAN)[None, :]
    kcol = key_col[:, None, :]
    valid = (kcol >= win_start[:, :, None]) & (kcol < win_start[:, :, None] + NA_KW)
    dc_idx = np.clip(kcol - q_col[:, :, None] + NA_KW - 1, 0, 2 * NA_KW - 2)
    return key_col, valid, dc_idx


def neighbourhood_mixer(q, k, v, rpb):
    b, l, _ = q.shape
    rows = l // GRID_W
    kh = min(NA_KH_MAX, rows)
    key_col, valid, dc_idx = na_column_tables()
    qg = q.reshape(b, rows, N_COL_BLOCKS, NA_COL_BLOCK, TOK_HEADS, HEAD_DIM)
    kg = k.reshape(b, rows, GRID_W, TOK_HEADS, HEAD_DIM)[:, :, key_col]
    vg = v.reshape(b, rows, GRID_W, TOK_HEADS, HEAD_DIM)[:, :, key_col]
    rpb_f = rpb.astype(jnp.float32)
    scale = HEAD_DIM ** -0.5

    def row_block(r):
        rs = jnp.clip(r - kh // 2, 0, rows - kh)
        k_win = lax.dynamic_slice_in_dim(kg, rs, kh, axis=1)
        v_win = lax.dynamic_slice_in_dim(vg, rs, kh, axis=1)
        q_row = lax.dynamic_index_in_dim(qg, r, axis=1, keepdims=False)
        s = jnp.einsum('bcqhd,brckhd->bhcqrk', q_row, k_win).astype(jnp.float32) * scale
        dr_idx = rs + jnp.arange(kh) - r + (NA_KH_MAX - 1)
        bias = rpb_f[:, dr_idx][:, :, dc_idx].transpose(0, 2, 3, 1, 4)
        s = jnp.where(valid[:, :, None, :], s + bias[None], -jnp.inf)
        p = jax.nn.softmax(s, axis=(-2, -1)).astype(v_win.dtype)
        return jnp.einsum('bhcqrk,brckhd->bcqhd', p, v_win)

    out = lax.map(row_block, jnp.arange(rows))
    return out.transpose(1, 0, 2, 3, 4, 5).reshape(b, l, TOK_WIDTH)


def memory_attention(q, mem_k, mem_v):
    s = jnp.einsum('blhd,bmhd->bhlm', q, mem_k).astype(jnp.float32) * (HEAD_DIM ** -0.5)
    p = jax.nn.softmax(s, axis=-1).astype(mem_v.dtype)
    return jnp.einsum('bhlm,bmhd->blhd', p, mem_v)


def run_trunk(x, mem, norm_gain, mem_norm_gain, w_mem_kv, w_out, w_mlp_in, w_mlp_out,
              w_in_ret, ret_decay, w_in_na, na_rpb):
    b, l, _ = x.shape
    for i in range(DEPTH):
        g = norm_gain[i]
        h = rms_norm(x, g[0])
        m = rms_norm(mem, mem_norm_gain[i])
        mem_k, mem_v = jnp.split(m @ w_mem_kv[i], 2, axis=-1)
        mem_k = mem_k.reshape(b, N_MEM, MEM_HEADS, HEAD_DIM)
        mem_v = mem_v.reshape(b, N_MEM, MEM_HEADS, HEAD_DIM)
        j = i // N_MIXERS
        if i % N_MIXERS == 0:
            q, k, v, gate, q_mem = jnp.split(h @ w_in_ret[j], [TOK_WIDTH, 2 * TOK_WIDTH, 3 * TOK_WIDTH, 4 * TOK_WIDTH], axis=-1)
            tok = retention_mixer(q, k, v, gate, ret_decay[j])
        else:
            q, k, v, q_mem = jnp.split(h @ w_in_na[j], [TOK_WIDTH, 2 * TOK_WIDTH, 3 * TOK_WIDTH], axis=-1)
            tok = neighbourhood_mixer(q, k, v, na_rpb[j])
        mem_out = memory_attention(q_mem.reshape(b, l, MEM_HEADS, HEAD_DIM), mem_k, mem_v).reshape(b, l, MEM_WIDTH)
        mixed = jnp.concatenate([tok, mem_out], axis=-1) @ w_out[i]
        x = x + rms_norm(mixed, g[1])
        u = jnp.square(jax.nn.relu(rms_norm(x, g[2]) @ w_mlp_in[i]))
        x = x + rms_norm(u @ w_mlp_out[i], g[3])
    return x


def setup_inputs(seed: int = 0) -> dict:
    key = jax.random.key(seed)
    ks = jax.random.split(key, 14)
    normal = lambda k, shape: jax.random.normal(k, shape, jnp.float32)
    lin = lambda k, shape, fan_in: normal(k, shape) * (fan_in ** -0.5)
    decay_init = RET_DECAY_BASE + jnp.arange(TOK_HEADS, dtype=jnp.float32)[None, None, :]
    return {
        'x_prompt': normal(ks[0], (BATCH, SEQ, D_MODEL)),
        'x_sample': normal(ks[1], (DEC_BATCH, DEC_SEQ, D_MODEL)),
        'mem_prompt': normal(ks[2], (BATCH, N_MEM, D_MODEL)),
        'mem_sample': normal(ks[3], (DEC_BATCH, N_MEM, D_MODEL)),
        'norm_gain': 1.0 + 0.02 * normal(ks[4], (DEPTH, 4, D_MODEL)),
        'mem_norm_gain': 1.0 + 0.02 * normal(ks[5], (DEPTH, D_MODEL)),
        'w_mem_kv': lin(ks[6], (DEPTH, D_MODEL, 2 * MEM_WIDTH), D_MODEL),
        'w_out': lin(ks[7], (DEPTH, MIX_WIDTH, D_MODEL), MIX_WIDTH),
        'w_mlp_in': lin(ks[8], (DEPTH, D_MODEL, D_FF), D_MODEL),
        'w_mlp_out': lin(ks[9], (DEPTH, D_FF, D_MODEL), D_FF),
        'w_in_ret': lin(ks[10], (N_RET, D_MODEL, 4 * TOK_WIDTH + MEM_WIDTH), D_MODEL),
        'ret_decay': decay_init + 0.1 * normal(ks[11], (N_RET, 2, TOK_HEADS)),
        'w_in_na': lin(ks[12], (N_NA, D_MODEL, 3 * TOK_WIDTH + MEM_WIDTH), D_MODEL),
        'na_rpb': 0.05 * normal(ks[13], (N_NA, TOK_HEADS, 2 * NA_KH_MAX - 1, 2 * NA_KW - 1)),
    }


def reference(x_prompt, x_sample, mem_prompt, mem_sample, norm_gain, mem_norm_gain, w_mem_kv, w_out,
              w_mlp_in, w_mlp_out, w_in_ret, ret_decay, w_in_na, na_rpb):
    y_prompt = run_trunk(x_prompt, mem_prompt, norm_gain, mem_norm_gain, w_mem_kv, w_out, w_mlp_in, w_mlp_out,
                         w_in_ret, ret_decay, w_in_na, na_rpb)
    y_sample = run_trunk(x_sample, mem_sample, norm_gain, mem_norm_gain, w_mem_kv, w_out, w_mlp_in, w_mlp_out,
                         w_in_ret, ret_decay, w_in_na, na_rpb)
    return (y_prompt, y_sample)
```

```python
import functools

import jax
import jax.numpy as jnp
import numpy as np
from jax import lax
from jax.experimental import pallas as pl
from jax.experimental.pallas import tpu as pltpu

HEAD_DIM = 128
MEM_HEADS = 4
MEM_WIDTH = MEM_HEADS * HEAD_DIM
N_MIXERS = 2
RET_CHUNK = 128
ROPE_BASE = 10000.0
GRID_W = 64
NA_KH = 8
NA_KW = 16
NORM_EPS = 1e-6
ATTN_SCALE = HEAD_DIM ** -0.5

V7X_VMEM_BYTES = 64 * 1024 * 1024
VMEM_LIMIT_BYTES = V7X_VMEM_BYTES - 8 * 1024 * 1024

F32 = jnp.float32
BF16 = jnp.bfloat16
NT_DIMS = (((1,), (1,)), ((), ()))


def _params(*semantics):
    return pltpu.CompilerParams(dimension_semantics=semantics, vmem_limit_bytes=VMEM_LIMIT_BYTES)


def _rms(x, gain):
    return x * lax.rsqrt(jnp.mean(x * x, axis=-1, keepdims=True) + NORM_EPS) * gain


def _rms_rows(src_ref, gain_ref, dst_ref, residual_ref=None, rows_per_step=128):
    n_rows = src_ref.shape[0]
    step = min(rows_per_step, n_rows)
    gain = gain_ref[...]

    def body(c, carry):
        r0 = pl.multiple_of(c * step, step)
        y = _rms(src_ref[pl.ds(r0, step), :].astype(F32), gain)
        if residual_ref is not None:
            y = residual_ref[pl.ds(r0, step), :] + y
        dst_ref[pl.ds(r0, step), :] = y.astype(dst_ref.dtype)
        return carry

    lax.fori_loop(0, n_rows // step, body, 0)


def _norm_matmul_kernel(x_ref, g_ref, w_ref, o_ref, h_ref):
    @pl.when(pl.program_id(1) == 0)
    def _():
        _rms_rows(x_ref, g_ref, h_ref)

    o_ref[...] = jnp.dot(h_ref[...], w_ref[...], preferred_element_type=F32).astype(o_ref.dtype)


def _norm_matmul(x, gain, w, *, tm=1024, tn=512):
    t, d = x.shape
    n = w.shape[1]
    tm = min(tm, t)
    tn = min(tn, n)
    assert t % tm == 0 and n % tn == 0
    return pl.pallas_call(
        _norm_matmul_kernel,
        grid=(t // tm, n // tn),
        in_specs=[
            pl.BlockSpec((tm, d), lambda i, j: (i, 0)),
            pl.BlockSpec((1, d), lambda i, j: (0, 0)),
            pl.BlockSpec((d, tn), lambda i, j: (0, j)),
        ],
        out_specs=pl.BlockSpec((tm, tn), lambda i, j: (i, j)),
        out_shape=jax.ShapeDtypeStruct((t, n), BF16),
        scratch_shapes=[pltpu.VMEM((tm, d), BF16)],
        compiler_params=_params("parallel", "arbitrary"),
        name="norm_matmul",
    )(x, gain.reshape(1, d), w)


def _out_proj_kernel(tok_ref, mem_ref, x_ref, g_ref, wt_ref, wm_ref, o_ref):
    o_ref[...] = (jnp.dot(tok_ref[...], wt_ref[...], preferred_element_type=F32)
                  + jnp.dot(mem_ref[...], wm_ref[...], preferred_element_type=F32))
    _rms_rows(o_ref, g_ref, o_ref, residual_ref=x_ref)


def _out_proj(tok, mem_out, x, gain, w_tok, w_mem, *, tm=512):
    t, d = x.shape
    tm = min(tm, t)
    assert t % tm == 0
    tw, mw = tok.shape[1], mem_out.shape[1]
    return pl.pallas_call(
        _out_proj_kernel,
        grid=(t // tm,),
        in_specs=[
            pl.BlockSpec((tm, tw), lambda i: (i, 0)),
            pl.BlockSpec((tm, mw), lambda i: (i, 0)),
            pl.BlockSpec((tm, d), lambda i: (i, 0)),
            pl.BlockSpec((1, d), lambda i: (0, 0)),
            pl.BlockSpec((tw, d), lambda i: (0, 0)),
            pl.BlockSpec((mw, d), lambda i: (0, 0)),
        ],
        out_specs=pl.BlockSpec((tm, d), lambda i: (i, 0)),
        out_shape=jax.ShapeDtypeStruct((t, d), F32),
        compiler_params=_params("parallel"),
        name="out_proj",
    )(tok, mem_out, x, gain.reshape(1, d), w_tok, w_mem)


def _mlp_kernel(x_ref, gi_ref, go_ref, w1_ref, w2_ref, o_ref, h_ref):
    j = pl.program_id(1)

    @pl.when(j == 0)
    def _():
        _rms_rows(x_ref, gi_ref, h_ref)

    u = jnp.dot(h_ref[...], w1_ref[...], preferred_element_type=F32)
    u = jnp.square(jnp.maximum(u, 0.0)).astype(BF16)
    part = jnp.dot(u, w2_ref[...], preferred_element_type=F32)

    @pl.when(j == 0)
    def _():
        o_ref[...] = part

    @pl.when(j > 0)
    def _():
        o_ref[...] += part

    @pl.when(j == pl.num_programs(1) - 1)
    def _():
        _rms_rows(o_ref, go_ref, o_ref, residual_ref=x_ref)


def _mlp(x, gain_in, gain_out, w1, w2, *, tm=1024, tf=512):
    t, d = x.shape
    f = w1.shape[1]
    tm = min(tm, t)
    assert t % tm == 0 and f % tf == 0
    return pl.pallas_call(
        _mlp_kernel,
        grid=(t // tm, f // tf),
        in_specs=[
            pl.BlockSpec((tm, d), lambda i, j: (i, 0)),
            pl.BlockSpec((1, d), lambda i, j: (0, 0)),
            pl.BlockSpec((1, d), lambda i, j: (0, 0)),
            pl.BlockSpec((d, tf), lambda i, j: (0, j)),
            pl.BlockSpec((tf, d), lambda i, j: (j, 0)),
        ],
        out_specs=pl.BlockSpec((tm, d), lambda i, j: (i, 0)),
        out_shape=jax.ShapeDtypeStruct((t, d), F32),
        scratch_shapes=[pltpu.VMEM((tm, d), BF16)],
        compiler_params=_params("parallel", "arbitrary"),
        name="mlp",
    )(x, gain_in.reshape(1, d), gain_out.reshape(1, d), w1, w2)


def _mem_attn_kernel(q_ref, k_ref, v_ref, o_ref):
    for h in range(MEM_HEADS):
        cols = slice(h * HEAD_DIM, (h + 1) * HEAD_DIM)
        s = lax.dot_general(q_ref[:, cols], k_ref[:, cols], NT_DIMS, preferred_element_type=F32) * ATTN_SCALE
        p = jnp.exp(s - jnp.max(s, axis=-1, keepdims=True))
        denom = jnp.sum(p, axis=-1, keepdims=True)
        out = jnp.dot(p.astype(BF16), v_ref[:, cols], preferred_element_type=F32) / denom
        o_ref[:, cols] = out.astype(o_ref.dtype)


def _mem_attn(proj, mem_kv, *, tq=512):
    b, l, n = proj.shape
    m = mem_kv.shape[1]
    tq = min(tq, l)
    assert l % tq == 0 and n % MEM_WIDTH == 0
    q_block = n // MEM_WIDTH - 1
    return pl.pallas_call(
        _mem_attn_kernel,
        grid=(b, l // tq),
        in_specs=[
            pl.BlockSpec((None, tq, MEM_WIDTH), lambda bi, i: (bi, i, q_block)),
            pl.BlockSpec((None, m, MEM_WIDTH), lambda bi, i: (bi, 0, 0)),
            pl.BlockSpec((None, m, MEM_WIDTH), lambda bi, i: (bi, 0, 1)),
        ],
        out_specs=pl.BlockSpec((None, tq, MEM_WIDTH), lambda bi, i: (bi, i, 0)),
        out_shape=jax.ShapeDtypeStruct((b, l, MEM_WIDTH), BF16),
        compiler_params=_params("parallel", "parallel"),
        name="mem_attn",
    )(proj, mem_kv, mem_kv)


def _rotate(x, cos2, sin2):
    return x * cos2 + pltpu.roll(x, HEAD_DIM // 2, axis=1) * sin2


def _retention_kernel(dec_ref, cos_ref, sin_ref, q_ref, k_ref, v_ref, g_ref, o_ref, y_ref, *, n_chunks):
    c = RET_CHUNK
    log_gamma = jnp.log1p(-jnp.exp2(-dec_ref[...]))
    lgf, lgb = log_gamma[0:1, :], log_gamma[1:2, :]
    row = lax.broadcasted_iota(jnp.int32, (c, c), 0).astype(F32)
    col = lax.broadcasted_iota(jnp.int32, (c, c), 1).astype(F32)
    diff = row - col
    decay_intra = jnp.exp(jnp.where(diff >= 0, diff * lgf, -diff * lgb))
    pos = lax.broadcasted_iota(jnp.int32, (c, HEAD_DIM), 0).astype(F32)
    q_decay_f = jnp.exp((pos + 1.0) * lgf)
    k_decay_f = jnp.exp((c - 1.0 - pos) * lgf)
    q_decay_b = jnp.exp((c - pos) * lgb)
    k_decay_b = jnp.exp(pos * lgb)
    chunk_decay_f = jnp.exp(c * lgf)
    chunk_decay_b = jnp.exp(c * lgb)

    def load_qkv(off):
        cos2 = cos_ref[pl.ds(off, c), :]
        sin2 = sin_ref[pl.ds(off, c), :]
        qr = _rotate(q_ref[pl.ds(off, c), :].astype(F32), cos2, sin2)
        kr = _rotate(k_ref[pl.ds(off, c), :].astype(F32), cos2, sin2) * ATTN_SCALE
        return qr, kr, v_ref[pl.ds(off, c), :]

    def backward_step(t, state):
        off = pl.multiple_of((n_chunks - 1 - t) * c, c)
        qr, kr, v = load_qkv(off)
        y_ref[pl.ds(off, c), :] = jnp.dot((qr * q_decay_b).astype(BF16), state.astype(BF16),
                                          preferred_element_type=F32)
        kv = jnp.dot((kr * k_decay_b).T.astype(BF16), v, preferred_element_type=F32)
        return chunk_decay_b * state + kv

    lax.fori_loop(0, n_chunks, backward_step, jnp.zeros((HEAD_DIM, HEAD_DIM), F32))

    def forward_step(n, state):
        off = pl.multiple_of(n * c, c)
        qr, kr, v = load_qkv(off)
        scores = lax.dot_general(qr.astype(BF16), kr.astype(BF16), NT_DIMS, preferred_element_type=F32)
        y = jnp.dot((scores * decay_intra).astype(BF16), v, preferred_element_type=F32)
        y = y + jnp.dot((qr * q_decay_f).astype(BF16), state.astype(BF16), preferred_element_type=F32)
        y = y + y_ref[pl.ds(off, c), :]
        y = y * lax.rsqrt(jnp.mean(y * y, axis=-1, keepdims=True) + NORM_EPS)
        gate = g_ref[pl.ds(off, c), :].astype(F32)
        o_ref[pl.ds(off, c), :] = (gate * (1.0 / (1.0 + jnp.exp(-gate))) * y).astype(o_ref.dtype)
        kv = jnp.dot((kr * k_decay_f).T.astype(BF16), v, preferred_element_type=F32)
        return chunk_decay_f * state + kv

    lax.fori_loop(0, n_chunks, forward_step, jnp.zeros((HEAD_DIM, HEAD_DIM), F32))


def _rotary_tables(l):
    half = HEAD_DIM // 2
    inv_freq = ROPE_BASE ** (-jnp.arange(half, dtype=F32) / half)
    ang = jnp.arange(l, dtype=F32)[:, None] * inv_freq[None, :]
    cos, sin = jnp.cos(ang), jnp.sin(ang)
    return jnp.concatenate([cos, cos], axis=-1), jnp.concatenate([-sin, sin], axis=-1)


def _retention(proj, decay_exp, tok_heads):
    b, l, _ = proj.shape
    assert l % RET_CHUNK == 0
    cos2, sin2 = _rotary_tables(l)
    dec = jnp.broadcast_to(decay_exp.astype(F32).T[:, :, None], (tok_heads, 2, HEAD_DIM))
    head_spec = lambda part: pl.BlockSpec((None, l, HEAD_DIM), lambda bi, h: (bi, 0, part * tok_heads + h))
    table_spec = pl.BlockSpec((l, HEAD_DIM), lambda bi, h: (0, 0))
    return pl.pallas_call(
        functools.partial(_retention_kernel, n_chunks=l // RET_CHUNK),
        grid=(b, tok_heads),
        in_specs=[pl.BlockSpec((None, 2, HEAD_DIM), lambda bi, h: (h, 0, 0)), table_spec, table_spec,
                  head_spec(0), head_spec(1), head_spec(2), head_spec(3)],
        out_specs=pl.BlockSpec((None, l, HEAD_DIM), lambda bi, h: (bi, 0, h)),
        out_shape=jax.ShapeDtypeStruct((b, l, tok_heads * HEAD_DIM), BF16),
        scratch_shapes=[pltpu.VMEM((l, HEAD_DIM), F32)],
        compiler_params=_params("parallel", "parallel"),
        name="retention",
    )(dec, cos2, sin2, proj, proj, proj, proj)


def _na_kernel(bias_ref, q_ref, k_ref, v_ref, o_ref, *, rows):
    win = NA_KH * GRID_W

    def row_step(r, carry):
        rs = jnp.clip(r - NA_KH // 2, 0, rows - NA_KH)
        q_off = pl.multiple_of(r * GRID_W, GRID_W)
        k_off = pl.multiple_of(rs * GRID_W, GRID_W)
        s = lax.dot_general(q_ref[pl.ds(q_off, GRID_W), :], k_ref[pl.ds(k_off, win), :], NT_DIMS,
                            preferred_element_type=F32)
        s = s * ATTN_SCALE + bias_ref[rs - r + NA_KH - 1]
        p = jnp.exp(s - jnp.max(s, axis=-1, keepdims=True))
        denom = jnp.sum(p, axis=-1, keepdims=True)
        out = jnp.dot(p.astype(BF16), v_ref[pl.ds(k_off, win), :], preferred_element_type=F32) / denom
        o_ref[pl.ds(q_off, GRID_W), :] = out.astype(o_ref.dtype)
        return carry

    lax.fori_loop(0, rows, row_step, 0)


def _na_bias_table(rpb):
    c = np.arange(GRID_W)
    win_start = np.clip(c - NA_KW // 2, 0, GRID_W - NA_KW)
    valid = (c[None, :] >= win_start[:, None]) & (c[None, :] < win_start[:, None] + NA_KW)
    dc_idx = np.clip(c[None, :] - c[:, None] + NA_KW - 1, 0, 2 * NA_KW - 2)
    toeplitz = jnp.where(valid[None, None], rpb.astype(F32)[:, :, dc_idx], -jnp.inf)
    dr_idx = np.arange(NA_KH)[:, None] + np.arange(NA_KH)[None, :]
    table = toeplitz[:, dr_idx]
    table = table.transpose(0, 1, 3, 2, 4)
    return table.reshape(rpb.shape[0], NA_KH, GRID_W, NA_KH * GRID_W)


def _neighbourhood(proj, rpb, tok_heads):
    b, l, _ = proj.shape
    rows = l // GRID_W
    assert l % GRID_W == 0 and rows >= NA_KH
    bias = _na_bias_table(rpb)
    head_spec = lambda part: pl.BlockSpec((None, l, HEAD_DIM), lambda bi, h: (bi, 0, part * tok_heads + h))
    return pl.pallas_call(
        functools.partial(_na_kernel, rows=rows),
        grid=(b, tok_heads),
        in_specs=[pl.BlockSpec((None, NA_KH, GRID_W, NA_KH * GRID_W), lambda bi, h: (h, 0, 0, 0)),
                  head_spec(0), head_spec(1), head_spec(2)],
        out_specs=pl.BlockSpec((None, l, HEAD_DIM), lambda bi, h: (bi, 0, h)),
        out_shape=jax.ShapeDtypeStruct((b, l, tok_heads * HEAD_DIM), BF16),
        compiler_params=_params("parallel", "parallel"),
        name="neighbourhood",
    )(bias, proj, proj, proj)


def _trunk(x, mem, norm_gain, mem_norm_gain, w_mem_kv, w_out, w_mlp_in, w_mlp_out, w_in_ret, ret_decay,
           w_in_na, na_rpb):
    b, l, d = x.shape
    n_mem = mem.shape[1]
    tok_width = w_out.shape[1] - MEM_WIDTH
    tok_heads = tok_width // HEAD_DIM
    xf = x.reshape(b * l, d)
    memf = mem.reshape(b * n_mem, d)
    for i in range(norm_gain.shape[0]):
        g = norm_gain[i]
        mem_kv = _norm_matmul(memf, mem_norm_gain[i], w_mem_kv[i]).reshape(b, n_mem, 2 * MEM_WIDTH)
        j = i // N_MIXERS
        if i % N_MIXERS == 0:
            proj = _norm_matmul(xf, g[0], w_in_ret[j]).reshape(b, l, -1)
            tok = _retention(proj, ret_decay[j], tok_heads)
        else:
            proj = _norm_matmul(xf, g[0], w_in_na[j]).reshape(b, l, -1)
            tok = _neighbourhood(proj, na_rpb[j], tok_heads)
        mem_out = _mem_attn(proj, mem_kv)
        x1 = _out_proj(tok.reshape(b * l, tok_width), mem_out.reshape(b * l, MEM_WIDTH), xf, g[1],
                       w_out[i, :tok_width], w_out[i, tok_width:])
        xf = _mlp(x1, g[2], g[3], w_mlp_in[i], w_mlp_out[i])
    return xf.reshape(b, l, d)


def kernel(x_prompt, x_sample, mem_prompt, mem_sample, norm_gain, mem_norm_gain, w_mem_kv, w_out, w_mlp_in,
           w_mlp_out, w_in_ret, ret_decay, w_in_na, na_rpb):
    weights = [w.astype(BF16) for w in (w_mem_kv, w_out, w_mlp_in, w_mlp_out, w_in_ret)]
    w_mem_kv, w_out, w_mlp_in, w_mlp_out, w_in_ret = weights
    w_in_na = w_in_na.astype(BF16)
    run = functools.partial(_trunk, norm_gain=norm_gain, mem_norm_gain=mem_norm_gain, w_mem_kv=w_mem_kv,
                            w_out=w_out, w_mlp_in=w_mlp_in, w_mlp_out=w_mlp_out, w_in_ret=w_in_ret,
                            ret_decay=ret_decay, w_in_na=w_in_na, na_rpb=na_rpb)
    return (run(x_prompt, mem_prompt), run(x_sample, mem_sample))
```

```python
import functools

import jax
import jax.numpy as jnp
import numpy as np
from jax import lax
from jax.experimental import pallas as pl
from jax.experimental.pallas import tpu as pltpu

HEAD_DIM = 128
MEM_HEADS = 4
MEM_WIDTH = MEM_HEADS * HEAD_DIM
N_MIXERS = 2
RET_CHUNK = 128
ROPE_BASE = 10000.0
GRID_W = 64
NA_KH = 8
NA_KW = 16
NORM_EPS = 1e-6
ATTN_SCALE = HEAD_DIM ** -0.5
RET_UNROLL = 8
NA_UNROLL = 8

V7X_VMEM_BYTES = 64 * 1024 * 1024
VMEM_LIMIT_BYTES = V7X_VMEM_BYTES - 8 * 1024 * 1024

F32 = jnp.float32
BF16 = jnp.bfloat16
NT_DIMS = (((1,), (1,)), ((), ()))


def _params(*semantics):
    return pltpu.CompilerParams(dimension_semantics=semantics, vmem_limit_bytes=VMEM_LIMIT_BYTES)


def _rms(x, gain):
    return x * lax.rsqrt(jnp.mean(x * x, axis=-1, keepdims=True) + NORM_EPS) * gain


def _rms_rows(src_ref, gain_ref, dst_ref, residual_ref=None, rows_per_step=128):
    n_rows = src_ref.shape[0]
    step = min(rows_per_step, n_rows)
    gain = gain_ref[...]

    def body(c, carry):
        r0 = pl.multiple_of(c * step, step)
        y = _rms(src_ref[pl.ds(r0, step), :].astype(F32), gain)
        if residual_ref is not None:
            y = residual_ref[pl.ds(r0, step), :] + y
        dst_ref[pl.ds(r0, step), :] = y.astype(dst_ref.dtype)
        return carry

    lax.fori_loop(0, n_rows // step, body, 0)


def _norm_matmul_kernel(x_ref, g_ref, w_ref, o_ref, *, sub_rows):
    gain = g_ref[...]
    for r in range(x_ref.shape[0] // sub_rows):
        rows = slice(r * sub_rows, (r + 1) * sub_rows)
        h = _rms(x_ref[rows, :], gain).astype(BF16)
        o_ref[rows, :] = jnp.dot(h, w_ref[...], preferred_element_type=F32).astype(o_ref.dtype)


def _norm_matmul(x, gain, w, *, tm=512, sub_rows=256):
    t, d = x.shape
    n = w.shape[1]
    tm = min(tm, t)
    sub_rows = min(sub_rows, tm)
    assert t % tm == 0 and tm % sub_rows == 0
    return pl.pallas_call(
        functools.partial(_norm_matmul_kernel, sub_rows=sub_rows),
        grid=(t // tm,),
        in_specs=[
            pl.BlockSpec((tm, d), lambda i: (i, 0)),
            pl.BlockSpec((1, d), lambda i: (0, 0)),
            pl.BlockSpec((d, n), lambda i: (0, 0), pipeline_mode=pl.Buffered(1)),
        ],
        out_specs=pl.BlockSpec((tm, n), lambda i: (i, 0)),
        out_shape=jax.ShapeDtypeStruct((t, n), BF16),
        compiler_params=_params("parallel"),
        name="norm_matmul",
    )(x, gain.reshape(1, d), w)


def _out_proj_kernel(tok_ref, mem_ref, x_ref, g_ref, wt_ref, wm_ref, o_ref, *, sub_rows):
    gain = g_ref[...]
    for r in range(x_ref.shape[0] // sub_rows):
        rows = slice(r * sub_rows, (r + 1) * sub_rows)
        mixed = (jnp.dot(tok_ref[rows, :], wt_ref[...], preferred_element_type=F32)
                 + jnp.dot(mem_ref[rows, :], wm_ref[...], preferred_element_type=F32))
        o_ref[rows, :] = x_ref[rows, :] + _rms(mixed, gain)


def _out_proj(tok, mem_out, x, gain, w_tok, w_mem, *, tm=512, sub_rows=256):
    t, d = x.shape
    tm = min(tm, t)
    sub_rows = min(sub_rows, tm)
    assert t % tm == 0 and tm % sub_rows == 0
    tw, mw = tok.shape[1], mem_out.shape[1]
    return pl.pallas_call(
        functools.partial(_out_proj_kernel, sub_rows=sub_rows),
        grid=(t // tm,),
        in_specs=[
            pl.BlockSpec((tm, tw), lambda i: (i, 0)),
            pl.BlockSpec((tm, mw), lambda i: (i, 0)),
            pl.BlockSpec((tm, d), lambda i: (i, 0)),
            pl.BlockSpec((1, d), lambda i: (0, 0)),
            pl.BlockSpec((tw, d), lambda i: (0, 0)),
            pl.BlockSpec((mw, d), lambda i: (0, 0)),
        ],
        out_specs=pl.BlockSpec((tm, d), lambda i: (i, 0)),
        out_shape=jax.ShapeDtypeStruct((t, d), F32),
        compiler_params=_params("parallel"),
        name="out_proj",
    )(tok, mem_out, x, gain.reshape(1, d), w_tok, w_mem)


def _mlp_kernel(x_ref, gi_ref, go_ref, w1_ref, w2_ref, o_ref, h_ref):
    j = pl.program_id(1)

    @pl.when(j == 0)
    def _():
        _rms_rows(x_ref, gi_ref, h_ref)
        o_ref[...] = jnp.zeros_like(o_ref)

    u = jnp.dot(h_ref[...], w1_ref[...], preferred_element_type=F32)
    u = jnp.square(jnp.maximum(u, 0.0)).astype(BF16)
    o_ref[...] += jnp.dot(u, w2_ref[...], preferred_element_type=F32)

    @pl.when(j == pl.num_programs(1) - 1)
    def _():
        _rms_rows(o_ref, go_ref, o_ref, residual_ref=x_ref)


def _mlp(x, gain_in, gain_out, w1, w2, *, tm=1024, tf=1024):
    t, d = x.shape
    f = w1.shape[1]
    tm = min(tm, t)
    assert t % tm == 0 and f % tf == 0
    return pl.pallas_call(
        _mlp_kernel,
        grid=(t // tm, f // tf),
        in_specs=[
            pl.BlockSpec((tm, d), lambda i, j: (i, 0), pipeline_mode=pl.Buffered(1)),
            pl.BlockSpec((1, d), lambda i, j: (0, 0)),
            pl.BlockSpec((1, d), lambda i, j: (0, 0)),
            pl.BlockSpec((d, tf), lambda i, j: (0, j)),
            pl.BlockSpec((tf, d), lambda i, j: (j, 0)),
        ],
        out_specs=pl.BlockSpec((tm, d), lambda i, j: (i, 0)),
        out_shape=jax.ShapeDtypeStruct((t, d), F32),
        scratch_shapes=[pltpu.VMEM((tm, d), BF16)],
        compiler_params=_params("parallel", "arbitrary"),
        name="mlp",
    )(x, gain_in.reshape(1, d), gain_out.reshape(1, d), w1, w2)


def _mem_attn_kernel(q_ref, k_ref, v_ref, o_ref):
    for h in range(MEM_HEADS):
        cols = slice(h * HEAD_DIM, (h + 1) * HEAD_DIM)
        s = lax.dot_general(q_ref[:, cols], k_ref[:, cols], NT_DIMS, preferred_element_type=F32) * ATTN_SCALE
        p = jnp.exp(s - jnp.max(s, axis=-1, keepdims=True))
        denom = jnp.sum(p, axis=-1, keepdims=True)
        out = jnp.dot(p.astype(BF16), v_ref[:, cols], preferred_element_type=F32) / denom
        o_ref[:, cols] = out.astype(o_ref.dtype)


def _mem_attn(proj, mem_kv, *, tq=512):
    b, l, n = proj.shape
    m = mem_kv.shape[1]
    tq = min(tq, l)
    assert l % tq == 0 and n % MEM_WIDTH == 0
    q_block = n // MEM_WIDTH - 1
    return pl.pallas_call(
        _mem_attn_kernel,
        grid=(b, l // tq),
        in_specs=[
            pl.BlockSpec((None, tq, MEM_WIDTH), lambda bi, i: (bi, i, q_block)),
            pl.BlockSpec((None, m, MEM_WIDTH), lambda bi, i: (bi, 0, 0)),
            pl.BlockSpec((None, m, MEM_WIDTH), lambda bi, i: (bi, 0, 1)),
        ],
        out_specs=pl.BlockSpec((None, tq, MEM_WIDTH), lambda bi, i: (bi, i, 0)),
        out_shape=jax.ShapeDtypeStruct((b, l, MEM_WIDTH), BF16),
        compiler_params=_params("parallel", "parallel"),
        name="mem_attn",
    )(proj, mem_kv, mem_kv)


def _rotate(x, cos2, sin2):
    return x * cos2 + pltpu.roll(x, HEAD_DIM // 2, axis=1) * sin2


def _retention_kernel(dec_ref, cos_ref, sin_ref, q_ref, k_ref, v_ref, g_ref, o_ref,
                      kr_ref, kv_ref, sf_ref, sb_ref, *, n_chunks):
    c = RET_CHUNK
    unroll = min(RET_UNROLL, n_chunks)
    log_gamma = jnp.log1p(-jnp.exp2(-dec_ref[...]))
    lgf, lgb = log_gamma[0:1, :], log_gamma[1:2, :]
    lane = lax.broadcasted_iota(jnp.int32, (1, c), 1).astype(F32)
    k_decay_f_row = jnp.exp((c - 1.0 - lane) * lgf)
    k_decay_b_row = jnp.exp(lane * lgb)
    chunk_decay_f = jnp.exp(c * lgf)
    chunk_decay_b = jnp.exp(c * lgb)

    def kv_step(i, carry):
        offs = [pl.multiple_of((i * unroll + u) * c, c) for u in range(unroll)]
        lhs = []
        for off in offs:
            kr = _rotate(k_ref[pl.ds(off, c), :].astype(F32), cos_ref[pl.ds(off, c), :],
                         sin_ref[pl.ds(off, c), :]) * ATTN_SCALE
            kr_ref[pl.ds(off, c), :] = kr.astype(BF16)
            kr_t = kr.T
            lhs.append(jnp.concatenate([kr_t * k_decay_f_row, kr_t * k_decay_b_row], axis=0).astype(BF16))
        for u, off in enumerate(offs):
            kv_ref[i * unroll + u] = jnp.dot(lhs[u], v_ref[pl.ds(off, c), :], preferred_element_type=F32)
        return carry

    lax.fori_loop(0, n_chunks // unroll, kv_step, 0)

    def scan_step(t, carry):
        sf, sb = carry
        nb = n_chunks - 1 - t
        sf_ref[t] = sf.astype(BF16)
        sb_ref[nb] = sb.astype(BF16)
        sf = chunk_decay_f * sf + kv_ref[t, 0:HEAD_DIM, :]
        sb = chunk_decay_b * sb + kv_ref[nb, HEAD_DIM:2 * HEAD_DIM, :]
        return sf, sb

    zero_state = jnp.zeros((HEAD_DIM, HEAD_DIM), F32)
    lax.fori_loop(0, n_chunks, scan_step, (zero_state, zero_state), unroll=unroll)

    row = lax.broadcasted_iota(jnp.int32, (c, c), 0).astype(F32)
    col = lax.broadcasted_iota(jnp.int32, (c, c), 1).astype(F32)
    diff = row - col
    decay_intra = jnp.exp(jnp.where(diff >= 0, diff * lgf, -diff * lgb))
    pos = lax.broadcasted_iota(jnp.int32, (c, HEAD_DIM), 0).astype(F32)
    q_decay_f = jnp.exp((pos + 1.0) * lgf)
    q_decay_b = jnp.exp((c - pos) * lgb)

    def out_step(i, carry):
        chunks = [i * unroll + u for u in range(unroll)]
        offs = [pl.multiple_of(n * c, c) for n in chunks]
        qrs, scores, ys = [], [], []
        for off in offs:
            qr = _rotate(q_ref[pl.ds(off, c), :].astype(F32), cos_ref[pl.ds(off, c), :],
                         sin_ref[pl.ds(off, c), :])
            qrs.append(qr)
            scores.append(lax.dot_general(qr.astype(BF16), kr_ref[pl.ds(off, c), :], NT_DIMS,
                                          preferred_element_type=F32))
        for n, off, qr, s in zip(chunks, offs, qrs, scores):
            lhs = jnp.concatenate([s * decay_intra, qr * q_decay_f, qr * q_decay_b], axis=1).astype(BF16)
            rhs = jnp.concatenate([v_ref[pl.ds(off, c), :], sf_ref[n], sb_ref[n]], axis=0)
            ys.append(jnp.dot(lhs, rhs, preferred_element_type=F32))
        for off, y in zip(offs, ys):
            y = y * lax.rsqrt(jnp.mean(y * y, axis=-1, keepdims=True) + NORM_EPS)
            gate = g_ref[pl.ds(off, c), :].astype(F32)
            o_ref[pl.ds(off, c), :] = (gate * (1.0 / (1.0 + jnp.exp(-gate))) * y).astype(o_ref.dtype)
        return carry

    lax.fori_loop(0, n_chunks // unroll, out_step, 0)


def _rotary_tables(l):
    half = HEAD_DIM // 2
    inv_freq = ROPE_BASE ** (-jnp.arange(half, dtype=F32) / half)
    ang = jnp.arange(l, dtype=F32)[:, None] * inv_freq[None, :]
    cos, sin = jnp.cos(ang), jnp.sin(ang)
    return jnp.concatenate([cos, cos], axis=-1), jnp.concatenate([-sin, sin], axis=-1)


def _retention(proj, decay_exp, tok_heads):
    b, l, _ = proj.shape
    assert l % RET_CHUNK == 0 and (l // RET_CHUNK) % min(RET_UNROLL, l // RET_CHUNK) == 0
    cos2, sin2 = _rotary_tables(l)
    dec = jnp.broadcast_to(decay_exp.astype(F32).T[:, :, None], (tok_heads, 2, HEAD_DIM))
    head_spec = lambda part: pl.BlockSpec((None, l, HEAD_DIM), lambda bi, h: (bi, 0, part * tok_heads + h))
    table_spec = pl.BlockSpec((l, HEAD_DIM), lambda bi, h: (0, 0), pipeline_mode=pl.Buffered(1))
    n_chunks = l // RET_CHUNK
    return pl.pallas_call(
        functools.partial(_retention_kernel, n_chunks=n_chunks),
        grid=(b, tok_heads),
        in_specs=[pl.BlockSpec((None, 2, HEAD_DIM), lambda bi, h: (h, 0, 0)), table_spec, table_spec,
                  head_spec(0), head_spec(1), head_spec(2), head_spec(3)],
        out_specs=pl.BlockSpec((None, l, HEAD_DIM), lambda bi, h: (bi, 0, h)),
        out_shape=jax.ShapeDtypeStruct((b, l, tok_heads * HEAD_DIM), BF16),
        scratch_shapes=[pltpu.VMEM((l, HEAD_DIM), BF16),
                        pltpu.VMEM((n_chunks, 2 * HEAD_DIM, HEAD_DIM), F32),
                        pltpu.VMEM((n_chunks, HEAD_DIM, HEAD_DIM), BF16),
                        pltpu.VMEM((n_chunks, HEAD_DIM, HEAD_DIM), BF16)],
        compiler_params=_params("parallel", "parallel"),
        name="retention",
    )(dec, cos2, sin2, proj, proj, proj, proj)


def _na_kernel(bias_ref, q_ref, k_ref, v_ref, o_ref, *, rows):
    win = NA_KH * GRID_W

    unroll = min(NA_UNROLL, rows)

    def rows_step(i, carry):
        qrows = [i * unroll + u for u in range(unroll)]
        starts = [jnp.clip(r - NA_KH // 2, 0, rows - NA_KH) for r in qrows]
        q_offs = [pl.multiple_of(r * GRID_W, GRID_W) for r in qrows]
        k_offs = [pl.multiple_of(rs * GRID_W, GRID_W) for rs in starts]
        scores = [lax.dot_general(q_ref[pl.ds(q_off, GRID_W), :], k_ref[pl.ds(k_off, win), :], NT_DIMS,
                                  preferred_element_type=F32)
                  for q_off, k_off in zip(q_offs, k_offs)]
        probs, denoms = [], []
        for r, rs, s in zip(qrows, starts, scores):
            s = s * ATTN_SCALE + bias_ref[rs - r + NA_KH - 1]
            p = jnp.exp(s - jnp.max(s, axis=-1, keepdims=True))
            denoms.append(jnp.sum(p, axis=-1, keepdims=True))
            probs.append(p.astype(BF16))
        outs = [jnp.dot(p, v_ref[pl.ds(k_off, win), :], preferred_element_type=F32)
                for p, k_off in zip(probs, k_offs)]
        for q_off, out, denom in zip(q_offs, outs, denoms):
            o_ref[pl.ds(q_off, GRID_W), :] = (out / denom).astype(o_ref.dtype)
        return carry

    lax.fori_loop(0, rows // unroll, rows_step, 0)


def _na_bias_table(rpb):
    c = np.arange(GRID_W)
    win_start = np.clip(c - NA_KW // 2, 0, GRID_W - NA_KW)
    valid = (c[None, :] >= win_start[:, None]) & (c[None, :] < win_start[:, None] + NA_KW)
    dc_idx = np.clip(c[None, :] - c[:, None] + NA_KW - 1, 0, 2 * NA_KW - 2)
    toeplitz = jnp.where(valid[None, None], rpb.astype(F32)[:, :, dc_idx], -jnp.inf)
    dr_idx = np.arange(NA_KH)[:, None] + np.arange(NA_KH)[None, :]
    table = toeplitz[:, dr_idx]
    table = table.transpose(0, 1, 3, 2, 4)
    return table.reshape(rpb.shape[0], NA_KH, GRID_W, NA_KH * GRID_W)


def _neighbourhood(proj, rpb, tok_heads):
    b, l, _ = proj.shape
    rows = l // GRID_W
    assert l % GRID_W == 0 and rows >= NA_KH and rows % min(NA_UNROLL, rows) == 0
    bias = _na_bias_table(rpb)
    head_spec = lambda part: pl.BlockSpec((None, l, HEAD_DIM), lambda bi, h: (bi, 0, part * tok_heads + h))
    return pl.pallas_call(
        functools.partial(_na_kernel, rows=rows),
        grid=(b, tok_heads),
        in_specs=[pl.BlockSpec((None, NA_KH, GRID_W, NA_KH * GRID_W), lambda bi, h: (h, 0, 0, 0)),
                  head_spec(0), head_spec(1), head_spec(2)],
        out_specs=pl.BlockSpec((None, l, HEAD_DIM), lambda bi, h: (bi, 0, h)),
        out_shape=jax.ShapeDtypeStruct((b, l, tok_heads * HEAD_DIM), BF16),
        compiler_params=_params("parallel", "parallel"),
        name="neighbourhood",
    )(bias, proj, proj, proj)


def _trunk(x, mem, norm_gain, mem_norm_gain, w_mem_kv, w_out, w_mlp_in, w_mlp_out, w_in_ret, ret_decay,
           w_in_na, na_rpb):
    b, l, d = x.shape
    n_mem = mem.shape[1]
    tok_width = w_out.shape[1] - MEM_WIDTH
    tok_heads = tok_width // HEAD_DIM
    xf = x.reshape(b * l, d)
    memf = mem.reshape(b * n_mem, d)
    for i in range(norm_gain.shape[0]):
        g = norm_gain[i]
        mem_kv = _norm_matmul(memf, mem_norm_gain[i], w_mem_kv[i]).reshape(b, n_mem, 2 * MEM_WIDTH)
        j = i // N_MIXERS
        if i % N_MIXERS == 0:
            proj = _norm_matmul(xf, g[0], w_in_ret[j]).reshape(b, l, -1)
            tok = _retention(proj, ret_decay[j], tok_heads)
        else:
            proj = _norm_matmul(xf, g[0], w_in_na[j]).reshape(b, l, -1)
            tok = _neighbourhood(proj, na_rpb[j], tok_heads)
        mem_out = _mem_attn(proj, mem_kv)
        x1 = _out_proj(tok.reshape(b * l, tok_width), mem_out.reshape(b * l, MEM_WIDTH), xf, g[1],
                       w_out[i, :tok_width], w_out[i, tok_width:])
        xf = _mlp(x1, g[2], g[3], w_mlp_in[i], w_mlp_out[i])
    return xf.reshape(b, l, d)


def kernel(x_prompt, x_sample, mem_prompt, mem_sample, norm_gain, mem_norm_gain, w_mem_kv, w_out, w_mlp_in,
           w_mlp_out, w_in_ret, ret_decay, w_in_na, na_rpb):
    weights = [w.astype(BF16) for w in (w_mem_kv, w_out, w_mlp_in, w_mlp_out, w_in_ret)]
    w_mem_kv, w_out, w_mlp_in, w_mlp_out, w_in_ret = weights
    w_in_na = w_in_na.astype(BF16)
    run = functools.partial(_trunk, norm_gain=norm_gain, mem_norm_gain=mem_norm_gain, w_mem_kv=w_mem_kv,
                            w_out=w_out, w_mlp_in=w_mlp_in, w_mlp_out=w_mlp_out, w_in_ret=w_in_ret,
                            ret_decay=ret_decay, w_in_na=w_in_na, na_rpb=na_rpb)
    return (run(x_prompt, mem_prompt), run(x_sample, mem_sample))
```

```python
import functools

import jax
import jax.numpy as jnp
import numpy as np
from jax import lax
from jax.experimental import pallas as pl
from jax.experimental.pallas import tpu as pltpu

HEAD_DIM = 128
MEM_HEADS = 4
MEM_WIDTH = MEM_HEADS * HEAD_DIM
N_MIXERS = 2
RET_CHUNK = 128
ROPE_BASE = 10000.0
GRID_W = 64
NA_KH = 8
NA_KW = 16
NORM_EPS = 1e-6
ATTN_SCALE = HEAD_DIM ** -0.5
LOG2_E = 1.4426950408889634
RET_UNROLL = 16
NA_UNROLL = 16

V7X_VMEM_BYTES = 64 * 1024 * 1024
VMEM_LIMIT_BYTES = V7X_VMEM_BYTES - 8 * 1024 * 1024

F32 = jnp.float32
BF16 = jnp.bfloat16
NT_DIMS = (((1,), (1,)), ((), ()))


def _params(*semantics):
    return pltpu.CompilerParams(dimension_semantics=semantics, vmem_limit_bytes=VMEM_LIMIT_BYTES)


def _rms(x, gain):
    return x * lax.rsqrt(jnp.mean(x * x, axis=-1, keepdims=True) + NORM_EPS) * gain


def _norm_matmul_kernel(x_ref, g_ref, w_ref, o_ref, *, sub_rows):
    gain = g_ref[...]
    for r in range(x_ref.shape[0] // sub_rows):
        rows = slice(r * sub_rows, (r + 1) * sub_rows)
        h = _rms(x_ref[rows, :], gain).astype(BF16)
        o_ref[rows, :] = jnp.dot(h, w_ref[...], preferred_element_type=F32).astype(o_ref.dtype)


def _norm_matmul(x, gain, w, *, tm=512, sub_rows=256):
    t, d = x.shape
    n = w.shape[1]
    tm = min(tm, t)
    sub_rows = min(sub_rows, tm)
    assert t % tm == 0 and tm % sub_rows == 0
    return pl.pallas_call(
        functools.partial(_norm_matmul_kernel, sub_rows=sub_rows),
        grid=(t // tm,),
        in_specs=[
            pl.BlockSpec((tm, d), lambda i: (i, 0)),
            pl.BlockSpec((1, d), lambda i: (0, 0)),
            pl.BlockSpec((d, n), lambda i: (0, 0), pipeline_mode=pl.Buffered(1)),
        ],
        out_specs=pl.BlockSpec((tm, n), lambda i: (i, 0)),
        out_shape=jax.ShapeDtypeStruct((t, n), BF16),
        compiler_params=_params("parallel"),
        name="norm_matmul",
    )(x, gain.reshape(1, d), w)


def _out_proj_kernel(tok_ref, mem_ref, x_ref, g_ref, wt_ref, wm_ref, o_ref, *, sub_rows):
    gain = g_ref[...]
    for r in range(x_ref.shape[0] // sub_rows):
        rows = slice(r * sub_rows, (r + 1) * sub_rows)
        mixed = (jnp.dot(tok_ref[rows, :], wt_ref[...], preferred_element_type=F32)
                 + jnp.dot(mem_ref[rows, :], wm_ref[...], preferred_element_type=F32))
        o_ref[rows, :] = x_ref[rows, :] + _rms(mixed, gain)


def _out_proj(tok, mem_out, x, gain, w_out, *, tm=512, sub_rows=256):
    t, d = x.shape
    tm = min(tm, t)
    sub_rows = min(sub_rows, tm)
    tw, mw = tok.shape[1], mem_out.shape[1]
    assert t % tm == 0 and tm % sub_rows == 0 and tw % mw == 0 and w_out.shape[0] == tw + mw
    return pl.pallas_call(
        functools.partial(_out_proj_kernel, sub_rows=sub_rows),
        grid=(t // tm,),
        in_specs=[
            pl.BlockSpec((tm, tw), lambda i: (i, 0)),
            pl.BlockSpec((tm, mw), lambda i: (i, 0)),
            pl.BlockSpec((tm, d), lambda i: (i, 0)),
            pl.BlockSpec((1, d), lambda i: (0, 0)),
            pl.BlockSpec((tw, d), lambda i: (0, 0)),
            pl.BlockSpec((mw, d), lambda i: (tw // mw, 0)),
        ],
        out_specs=pl.BlockSpec((tm, d), lambda i: (i, 0)),
        out_shape=jax.ShapeDtypeStruct((t, d), F32),
        compiler_params=_params("parallel"),
        name="out_proj",
    )(tok, mem_out, x, gain.reshape(1, d), w_out, w_out)


def _mlp_kernel(x_ref, gi_ref, go_ref, w1_ref, w2_ref, o_ref, h_ref, *, sub_rows):
    j = pl.program_id(1)
    last = pl.num_programs(1) - 1
    n_sub = x_ref.shape[0] // sub_rows

    def step(first, final):
        hidden = []
        for r in range(n_sub):
            rows = slice(r * sub_rows, (r + 1) * sub_rows)
            if first:
                h = _rms(x_ref[rows, :], gi_ref[...]).astype(BF16)
                h_ref[rows, :] = h
            else:
                h = h_ref[rows, :]
            u = jnp.dot(h, w1_ref[...], preferred_element_type=F32)
            hidden.append(jnp.square(jnp.maximum(u, 0.0)).astype(BF16))
        for r in range(n_sub):
            rows = slice(r * sub_rows, (r + 1) * sub_rows)
            acc = jnp.dot(hidden[r], w2_ref[...], preferred_element_type=F32)
            if not first:
                acc = o_ref[rows, :] + acc
            if final:
                acc = x_ref[rows, :] + _rms(acc, go_ref[...])
            o_ref[rows, :] = acc

    pl.when(j == 0)(lambda: step(True, False))
    pl.when(jnp.logical_and(j > 0, j < last))(lambda: step(False, False))
    pl.when(j == last)(lambda: step(False, True))


def _mlp(x, gain_in, gain_out, w1, w2, *, tm=1024, tf=1024, sub_rows=512):
    t, d = x.shape
    f = w1.shape[1]
    tm = min(tm, t)
    sub_rows = min(sub_rows, tm)
    assert t % tm == 0 and f % tf == 0 and tm % sub_rows == 0 and f // tf >= 2
    return pl.pallas_call(
        functools.partial(_mlp_kernel, sub_rows=sub_rows),
        grid=(t // tm, f // tf),
        in_specs=[
            pl.BlockSpec((tm, d), lambda i, j: (i, 0), pipeline_mode=pl.Buffered(1)),
            pl.BlockSpec((1, d), lambda i, j: (0, 0)),
            pl.BlockSpec((1, d), lambda i, j: (0, 0)),
            pl.BlockSpec((d, tf), lambda i, j: (0, j)),
            pl.BlockSpec((tf, d), lambda i, j: (j, 0)),
        ],
        out_specs=pl.BlockSpec((tm, d), lambda i, j: (i, 0)),
        out_shape=jax.ShapeDtypeStruct((t, d), F32),
        scratch_shapes=[pltpu.VMEM((tm, d), BF16)],
        compiler_params=_params("parallel", "arbitrary"),
        name="mlp",
    )(x, gain_in.reshape(1, d), gain_out.reshape(1, d), w1, w2)


def _mem_attn_kernel(q_ref, k_ref, v_ref, o_ref):
    heads = [slice(h * HEAD_DIM, (h + 1) * HEAD_DIM) for h in range(MEM_HEADS)]
    scores = [lax.dot_general(q_ref[:, cols], k_ref[:, cols], NT_DIMS, preferred_element_type=F32)
              for cols in heads]
    probs, denoms = [], []
    for s in scores:
        s = s * ATTN_SCALE
        p = jnp.exp(s - jnp.max(s, axis=-1, keepdims=True))
        denoms.append(jnp.sum(p, axis=-1, keepdims=True))
        probs.append(p.astype(BF16))
    outs = [jnp.dot(p, v_ref[:, cols], preferred_element_type=F32) for p, cols in zip(probs, heads)]
    for cols, out, denom in zip(heads, outs, denoms):
        o_ref[:, cols] = (out / denom).astype(o_ref.dtype)


def _mem_attn(proj, mem_kv, *, tq=1024):
    b, l, n = proj.shape
    m = mem_kv.shape[1]
    tq = min(tq, l)
    assert l % tq == 0 and n % MEM_WIDTH == 0
    q_block = n // MEM_WIDTH - 1
    return pl.pallas_call(
        _mem_attn_kernel,
        grid=(b, l // tq),
        in_specs=[
            pl.BlockSpec((None, tq, MEM_WIDTH), lambda bi, i: (bi, i, q_block)),
            pl.BlockSpec((None, m, MEM_WIDTH), lambda bi, i: (bi, 0, 0)),
            pl.BlockSpec((None, m, MEM_WIDTH), lambda bi, i: (bi, 0, 1)),
        ],
        out_specs=pl.BlockSpec((None, tq, MEM_WIDTH), lambda bi, i: (bi, i, 0)),
        out_shape=jax.ShapeDtypeStruct((b, l, MEM_WIDTH), BF16),
        compiler_params=_params("parallel", "parallel"),
        name="mem_attn",
    )(proj, mem_kv, mem_kv)


def _rotate(x, cos2, sin2):
    return x * cos2 + pltpu.roll(x, HEAD_DIM // 2, axis=1) * sin2


def _retention_kernel(dec_ref, cos_ref, sin_ref, q_ref, k_ref, v_ref, g_ref, o_ref,
                      kr_ref, kv_ref, sf_ref, sb_ref, *, n_chunks):
    c = RET_CHUNK
    unroll = min(RET_UNROLL, n_chunks)
    log_gamma = jnp.log1p(-jnp.exp2(-dec_ref[...]))
    lgf, lgb = log_gamma[0:1, :], log_gamma[1:2, :]
    lane = lax.broadcasted_iota(jnp.int32, (1, c), 1).astype(F32)
    k_decay_f_row = jnp.exp((c - 1.0 - lane) * lgf)
    k_decay_b_row = jnp.exp(lane * lgb)
    chunk_decay_f = jnp.exp(c * lgf)
    chunk_decay_b = jnp.exp(c * lgb)

    def kv_step(i, carry):
        offs = [pl.multiple_of((i * unroll + u) * c, c) for u in range(unroll)]
        lhs = []
        for off in offs:
            kr = _rotate(k_ref[pl.ds(off, c), :].astype(F32), cos_ref[pl.ds(off, c), :],
                         sin_ref[pl.ds(off, c), :]) * ATTN_SCALE
            kr_ref[pl.ds(off, c), :] = kr.astype(BF16)
            kr_t = kr.T
            lhs.append(jnp.concatenate([kr_t * k_decay_f_row, kr_t * k_decay_b_row], axis=0).astype(BF16))
        for u, off in enumerate(offs):
            kv_ref[i * unroll + u] = jnp.dot(lhs[u], v_ref[pl.ds(off, c), :], preferred_element_type=F32)
        return carry

    lax.fori_loop(0, n_chunks // unroll, kv_step, 0)

    def scan_step(t, carry):
        sf, sb = carry
        nb = n_chunks - 1 - t
        sf_ref[t] = sf.astype(BF16)
        sb_ref[nb] = sb.astype(BF16)
        sf = chunk_decay_f * sf + kv_ref[t, 0:HEAD_DIM, :]
        sb = chunk_decay_b * sb + kv_ref[nb, HEAD_DIM:2 * HEAD_DIM, :]
        return sf, sb

    zero_state = jnp.zeros((HEAD_DIM, HEAD_DIM), F32)
    lax.fori_loop(0, n_chunks, scan_step, (zero_state, zero_state), unroll=unroll)

    row = lax.broadcasted_iota(jnp.int32, (c, c), 0).astype(F32)
    col = lax.broadcasted_iota(jnp.int32, (c, c), 1).astype(F32)
    diff = row - col
    decay_intra = jnp.exp(jnp.where(diff >= 0, diff * lgf, -diff * lgb))
    pos = lax.broadcasted_iota(jnp.int32, (c, HEAD_DIM), 0).astype(F32)
    q_decay_f = jnp.exp((pos + 1.0) * lgf)
    q_decay_b = jnp.exp((c - pos) * lgb)

    def out_step(i, carry):
        chunks = [i * unroll + u for u in range(unroll)]
        offs = [pl.multiple_of(n * c, c) for n in chunks]
        qrs, scores, ys = [], [], []
        for off in offs:
            qr = _rotate(q_ref[pl.ds(off, c), :].astype(F32), cos_ref[pl.ds(off, c), :],
                         sin_ref[pl.ds(off, c), :])
            qrs.append(qr)
            scores.append(lax.dot_general(qr.astype(BF16), kr_ref[pl.ds(off, c), :], NT_DIMS,
                                          preferred_element_type=F32))
        for n, off, qr, s in zip(chunks, offs, qrs, scores):
            lhs = jnp.concatenate([s * decay_intra, qr * q_decay_f, qr * q_decay_b], axis=1).astype(BF16)
            rhs = jnp.concatenate([v_ref[pl.ds(off, c), :], sf_ref[n], sb_ref[n]], axis=0)
            ys.append(jnp.dot(lhs, rhs, preferred_element_type=F32))
        for off, y in zip(offs, ys):
            y = y * lax.rsqrt(jnp.mean(y * y, axis=-1, keepdims=True) + NORM_EPS)
            gate = g_ref[pl.ds(off, c), :].astype(F32)
            o_ref[pl.ds(off, c), :] = (gate * (1.0 / (1.0 + jnp.exp(-gate))) * y).astype(o_ref.dtype)
        return carry

    lax.fori_loop(0, n_chunks // unroll, out_step, 0)


def _rotary_tables(l):
    half = HEAD_DIM // 2
    inv_freq = ROPE_BASE ** (-jnp.arange(half, dtype=F32) / half)
    ang = jnp.arange(l, dtype=F32)[:, None] * inv_freq[None, :]
    cos, sin = jnp.cos(ang), jnp.sin(ang)
    return jnp.concatenate([cos, cos], axis=-1), jnp.concatenate([-sin, sin], axis=-1)


def _retention(proj, decay_exp, tok_heads):
    b, l, _ = proj.shape
    assert l % RET_CHUNK == 0 and (l // RET_CHUNK) % min(RET_UNROLL, l // RET_CHUNK) == 0
    cos2, sin2 = _rotary_tables(l)
    dec = jnp.broadcast_to(decay_exp.astype(F32).T[:, :, None], (tok_heads, 2, HEAD_DIM))
    head_spec = lambda part: pl.BlockSpec((None, l, HEAD_DIM), lambda bi, h: (bi, 0, part * tok_heads + h))
    table_spec = pl.BlockSpec((l, HEAD_DIM), lambda bi, h: (0, 0), pipeline_mode=pl.Buffered(1))
    n_chunks = l // RET_CHUNK
    return pl.pallas_call(
        functools.partial(_retention_kernel, n_chunks=n_chunks),
        grid=(b, tok_heads),
        in_specs=[pl.BlockSpec((None, 2, HEAD_DIM), lambda bi, h: (h, 0, 0)), table_spec, table_spec,
                  head_spec(0), head_spec(1), head_spec(2), head_spec(3)],
        out_specs=pl.BlockSpec((None, l, HEAD_DIM), lambda bi, h: (bi, 0, h)),
        out_shape=jax.ShapeDtypeStruct((b, l, tok_heads * HEAD_DIM), BF16),
        scratch_shapes=[pltpu.VMEM((l, HEAD_DIM), BF16),
                        pltpu.VMEM((n_chunks, 2 * HEAD_DIM, HEAD_DIM), F32),
                        pltpu.VMEM((n_chunks, HEAD_DIM, HEAD_DIM), BF16),
                        pltpu.VMEM((n_chunks, HEAD_DIM, HEAD_DIM), BF16)],
        compiler_params=_params("parallel", "parallel"),
        name="retention",
    )(dec, cos2, sin2, proj, proj, proj, proj)


def _na_kernel(bias_ref, q_ref, k_ref, v_ref, o_ref, *, rows):
    win = NA_KH * GRID_W

    unroll = min(NA_UNROLL, rows)

    def rows_step(i, carry):
        qrows = [i * unroll + u for u in range(unroll)]
        starts = [jnp.clip(r - NA_KH // 2, 0, rows - NA_KH) for r in qrows]
        q_offs = [pl.multiple_of(r * GRID_W, GRID_W) for r in qrows]
        k_offs = [pl.multiple_of(rs * GRID_W, GRID_W) for rs in starts]
        scores = [lax.dot_general(q_ref[pl.ds(q_off, GRID_W), :], k_ref[pl.ds(k_off, win), :], NT_DIMS,
                                  preferred_element_type=F32)
                  for q_off, k_off in zip(q_offs, k_offs)]
        probs, denoms = [], []
        for r, rs, s in zip(qrows, starts, scores):
            s = s * (ATTN_SCALE * LOG2_E) + bias_ref[rs - r + NA_KH - 1]
            p = jnp.exp2(s - jnp.max(s, axis=-1, keepdims=True))
            denoms.append(jnp.sum(p, axis=-1, keepdims=True))
            probs.append(p.astype(BF16))
        outs = [jnp.dot(p, v_ref[pl.ds(k_off, win), :], preferred_element_type=F32)
                for p, k_off in zip(probs, k_offs)]
        for q_off, out, denom in zip(q_offs, outs, denoms):
            o_ref[pl.ds(q_off, GRID_W), :] = (out / denom).astype(o_ref.dtype)
        return carry

    lax.fori_loop(0, rows // unroll, rows_step, 0)


def _na_bias_table(rpb):
    c = np.arange(GRID_W)
    win_start = np.clip(c - NA_KW // 2, 0, GRID_W - NA_KW)
    valid = (c[None, :] >= win_start[:, None]) & (c[None, :] < win_start[:, None] + NA_KW)
    dc_idx = np.clip(c[None, :] - c[:, None] + NA_KW - 1, 0, 2 * NA_KW - 2)
    toeplitz = jnp.where(valid[None, None], rpb.astype(F32)[:, :, dc_idx] * LOG2_E, -jnp.inf)
    dr_idx = np.arange(NA_KH)[:, None] + np.arange(NA_KH)[None, :]
    table = toeplitz[:, dr_idx]
    table = table.transpose(0, 1, 3, 2, 4)
    return table.reshape(rpb.shape[0], NA_KH, GRID_W, NA_KH * GRID_W)


def _neighbourhood(proj, rpb, tok_heads):
    b, l, _ = proj.shape
    rows = l // GRID_W
    assert l % GRID_W == 0 and rows >= NA_KH and rows % min(NA_UNROLL, rows) == 0
    bias = _na_bias_table(rpb)
    head_spec = lambda part: pl.BlockSpec((None, l, HEAD_DIM), lambda bi, h: (bi, 0, part * tok_heads + h))
    return pl.pallas_call(
        functools.partial(_na_kernel, rows=rows),
        grid=(b, tok_heads),
        in_specs=[pl.BlockSpec((None, NA_KH, GRID_W, NA_KH * GRID_W), lambda bi, h: (h, 0, 0, 0)),
                  head_spec(0), head_spec(1), head_spec(2)],
        out_specs=pl.BlockSpec((None, l, HEAD_DIM), lambda bi, h: (bi, 0, h)),
        out_shape=jax.ShapeDtypeStruct((b, l, tok_heads * HEAD_DIM), BF16),
        compiler_params=_params("parallel", "parallel"),
        name="neighbourhood",
    )(bias, proj, proj, proj)


def _trunk(x, mem, norm_gain, mem_norm_gain, w_mem_kv, w_out, w_mlp_in, w_mlp_out, w_in_ret, ret_decay,
           w_in_na, na_rpb):
    b, l, d = x.shape
    n_mem = mem.shape[1]
    tok_width = w_out.shape[1] - MEM_WIDTH
    tok_heads = tok_width // HEAD_DIM
    xf = x.reshape(b * l, d)
    memf = mem.reshape(b * n_mem, d)
    for i in range(norm_gain.shape[0]):
        g = norm_gain[i]
        mem_kv = _norm_matmul(memf, mem_norm_gain[i], w_mem_kv[i]).reshape(b, n_mem, 2 * MEM_WIDTH)
        j = i // N_MIXERS
        if i % N_MIXERS == 0:
            proj = _norm_matmul(xf, g[0], w_in_ret[j]).reshape(b, l, -1)
            tok = _retention(proj, ret_decay[j], tok_heads)
        else:
            proj = _norm_matmul(xf, g[0], w_in_na[j]).reshape(b, l, -1)
            tok = _neighbourhood(proj, na_rpb[j], tok_heads)
        mem_out = _mem_attn(proj, mem_kv)
        x1 = _out_proj(tok.reshape(b * l, tok_width), mem_out.reshape(b * l, MEM_WIDTH), xf, g[1], w_out[i])
        xf = _mlp(x1, g[2], g[3], w_mlp_in[i], w_mlp_out[i])
    return xf.reshape(b, l, d)


def kernel(x_prompt, x_sample, mem_prompt, mem_sample, norm_gain, mem_norm_gain, w_mem_kv, w_out, w_mlp_in,
           w_mlp_out, w_in_ret, ret_decay, w_in_na, na_rpb):
    weights = [w.astype(BF16) for w in (w_mem_kv, w_out, w_mlp_in, w_mlp_out, w_in_ret)]
    w_mem_kv, w_out, w_mlp_in, w_mlp_out, w_in_ret = weights
    w_in_na = w_in_na.astype(BF16)
    run = functools.partial(_trunk, norm_gain=norm_gain, mem_norm_gain=mem_norm_gain, w_mem_kv=w_mem_kv,
                            w_out=w_out, w_mlp_in=w_mlp_in, w_mlp_out=w_mlp_out, w_in_ret=w_in_ret,
                            ret_decay=ret_decay, w_in_na=w_in_na, na_rpb=na_rpb)
    return (run(x_prompt, mem_prompt), run(x_sample, mem_sample))
```

```python
import functools

import jax
import jax.numpy as jnp
import numpy as np
from jax import lax
from jax.experimental import pallas as pl
from jax.experimental.pallas import tpu as pltpu

HEAD_DIM = 128
MEM_HEADS = 4
MEM_WIDTH = MEM_HEADS * HEAD_DIM
N_MIXERS = 2
RET_CHUNK = 128
ROPE_BASE = 10000.0
GRID_W = 64
NA_KH = 8
NA_KW = 16
NORM_EPS = 1e-6
ATTN_SCALE = HEAD_DIM ** -0.5
LOG2_E = 1.4426950408889634
MLP_HIDDEN_TILE = 1024
RET_UNROLL = 16
NA_UNROLL = 16

V7X_VMEM_BYTES = 64 * 1024 * 1024
VMEM_LIMIT_BYTES = V7X_VMEM_BYTES - 8 * 1024 * 1024

F32 = jnp.float32
BF16 = jnp.bfloat16
NT_DIMS = (((1,), (1,)), ((), ()))


def _params(*semantics):
    return pltpu.CompilerParams(dimension_semantics=semantics, vmem_limit_bytes=VMEM_LIMIT_BYTES)


def _rms(x, gain):
    return x * lax.rsqrt(jnp.mean(x * x, axis=-1, keepdims=True) + NORM_EPS) * gain


def _norm_matmul_kernel(x_ref, g_ref, w_ref, o_ref, *, sub_rows):
    gain = g_ref[...]
    for r in range(x_ref.shape[0] // sub_rows):
        rows = slice(r * sub_rows, (r + 1) * sub_rows)
        h = _rms(x_ref[rows, :], gain).astype(BF16)
        o_ref[rows, :] = jnp.dot(h, w_ref[...], preferred_element_type=F32).astype(o_ref.dtype)


def _norm_matmul(x, gain, w, layer, *, tm=512, sub_rows=256):
    t, d = x.shape
    n = w.shape[2]
    tm = min(tm, t)
    sub_rows = min(sub_rows, tm)
    assert t % tm == 0 and tm % sub_rows == 0
    return pl.pallas_call(
        functools.partial(_norm_matmul_kernel, sub_rows=sub_rows),
        grid=(t // tm,),
        in_specs=[
            pl.BlockSpec((tm, d), lambda i: (i, 0)),
            pl.BlockSpec((1, d), lambda i: (0, 0)),
            pl.BlockSpec((None, d, n), lambda i: (layer, 0, 0), pipeline_mode=pl.Buffered(1)),
        ],
        out_specs=pl.BlockSpec((tm, n), lambda i: (i, 0)),
        out_shape=jax.ShapeDtypeStruct((t, n), BF16),
        compiler_params=_params("parallel"),
        name="norm_matmul",
    )(x, gain.reshape(1, d), w)


def _out_proj_kernel(tok_ref, mem_ref, x_ref, g_ref, wt_ref, wm_ref, o_ref, *, sub_rows):
    gain = g_ref[...]
    for r in range(x_ref.shape[0] // sub_rows):
        rows = slice(r * sub_rows, (r + 1) * sub_rows)
        mixed = (jnp.dot(tok_ref[rows, :], wt_ref[...], preferred_element_type=F32)
                 + jnp.dot(mem_ref[rows, :], wm_ref[...], preferred_element_type=F32))
        o_ref[rows, :] = x_ref[rows, :] + _rms(mixed, gain)


def _out_proj(tok, mem_out, x, gain, w_out, layer, *, tm=1024, sub_rows=256):
    t, d = x.shape
    tm = min(tm, t)
    sub_rows = min(sub_rows, tm)
    tw, mw = tok.shape[1], mem_out.shape[1]
    assert t % tm == 0 and tm % sub_rows == 0 and tw % mw == 0 and w_out.shape[1] == tw + mw
    return pl.pallas_call(
        functools.partial(_out_proj_kernel, sub_rows=sub_rows),
        grid=(t // tm,),
        in_specs=[
            pl.BlockSpec((tm, tw), lambda i: (i, 0)),
            pl.BlockSpec((tm, mw), lambda i: (i, 0)),
            pl.BlockSpec((tm, d), lambda i: (i, 0)),
            pl.BlockSpec((1, d), lambda i: (0, 0)),
            pl.BlockSpec((None, tw, d), lambda i: (layer, 0, 0), pipeline_mode=pl.Buffered(1)),
            pl.BlockSpec((None, mw, d), lambda i: (layer, tw // mw, 0), pipeline_mode=pl.Buffered(1)),
        ],
        out_specs=pl.BlockSpec((tm, d), lambda i: (i, 0)),
        out_shape=jax.ShapeDtypeStruct((t, d), F32),
        compiler_params=_params("parallel"),
        name="out_proj",
    )(tok, mem_out, x, gain.reshape(1, d), w_out, w_out)


def _mlp_kernel(x_ref, gi_ref, go_ref, w1_ref, w2_ref, o_ref, h_ref, *, sub_rows):
    j = pl.program_id(1)
    last = pl.num_programs(1) - 1
    n_sub = x_ref.shape[0] // sub_rows

    def step(first, final):
        hidden = []
        for r in range(n_sub):
            rows = slice(r * sub_rows, (r + 1) * sub_rows)
            if first:
                h = _rms(x_ref[rows, :], gi_ref[...]).astype(BF16)
                h_ref[rows, :] = h
            else:
                h = h_ref[rows, :]
            u = jnp.dot(h, w1_ref[...], preferred_element_type=F32)
            hidden.append(jnp.square(jnp.maximum(u, 0.0)).astype(BF16))
        for r in range(n_sub):
            rows = slice(r * sub_rows, (r + 1) * sub_rows)
            acc = jnp.dot(hidden[r], w2_ref[...], preferred_element_type=F32)
            if not first:
                acc = o_ref[rows, :] + acc
            if final:
                acc = x_ref[rows, :] + _rms(acc, go_ref[...])
            o_ref[rows, :] = acc

    pl.when(j == 0)(lambda: step(True, False))
    pl.when(jnp.logical_and(j > 0, j < last))(lambda: step(False, False))
    pl.when(j == last)(lambda: step(False, True))


def _tile_columns(w, tile):
    depth, d, f = w.shape
    return w.reshape(depth, d, f // tile, tile).transpose(0, 2, 1, 3)


def _mlp(x, gain_in, gain_out, w1_tiles, w2, layer, *, tm=1024, sub_rows=512):
    t, d = x.shape
    n_tiles, tf = w1_tiles.shape[1], w1_tiles.shape[3]
    tm = min(tm, t)
    sub_rows = min(sub_rows, tm)
    assert t % tm == 0 and tm % sub_rows == 0 and n_tiles >= 2 and w2.shape[1] == n_tiles * tf
    return pl.pallas_call(
        functools.partial(_mlp_kernel, sub_rows=sub_rows),
        grid=(t // tm, n_tiles),
        in_specs=[
            pl.BlockSpec((tm, d), lambda i, j: (i, 0), pipeline_mode=pl.Buffered(1)),
            pl.BlockSpec((1, d), lambda i, j: (0, 0)),
            pl.BlockSpec((1, d), lambda i, j: (0, 0)),
            pl.BlockSpec((None, None, d, tf), lambda i, j: (layer, j, 0, 0)),
            pl.BlockSpec((None, tf, d), lambda i, j: (layer, j, 0)),
        ],
        out_specs=pl.BlockSpec((tm, d), lambda i, j: (i, 0)),
        out_shape=jax.ShapeDtypeStruct((t, d), F32),
        scratch_shapes=[pltpu.VMEM((tm, d), BF16)],
        compiler_params=_params("parallel", "arbitrary"),
        name="mlp",
    )(x, gain_in.reshape(1, d), gain_out.reshape(1, d), w1_tiles, w2)


def _mem_attn_kernel(q_ref, k_ref, v_ref, o_ref):
    heads = [slice(h * HEAD_DIM, (h + 1) * HEAD_DIM) for h in range(MEM_HEADS)]
    scores = [lax.dot_general(q_ref[:, cols], k_ref[:, cols], NT_DIMS, preferred_element_type=F32)
              for cols in heads]
    probs, denoms = [], []
    for s in scores:
        s = s * ATTN_SCALE
        p = jnp.exp(s - jnp.max(s, axis=-1, keepdims=True))
        denoms.append(jnp.sum(p, axis=-1, keepdims=True))
        probs.append(p.astype(BF16))
    outs = [jnp.dot(p, v_ref[:, cols], preferred_element_type=F32) for p, cols in zip(probs, heads)]
    for cols, out, denom in zip(heads, outs, denoms):
        o_ref[:, cols] = (out / denom).astype(o_ref.dtype)


def _mem_attn(proj, mem_kv, *, tq=1024):
    b, l, n = proj.shape
    m = mem_kv.shape[1]
    tq = min(tq, l)
    assert l % tq == 0 and n % MEM_WIDTH == 0
    q_block = n // MEM_WIDTH - 1
    return pl.pallas_call(
        _mem_attn_kernel,
        grid=(b, l // tq),
        in_specs=[
            pl.BlockSpec((None, tq, MEM_WIDTH), lambda bi, i: (bi, i, q_block)),
            pl.BlockSpec((None, m, MEM_WIDTH), lambda bi, i: (bi, 0, 0)),
            pl.BlockSpec((None, m, MEM_WIDTH), lambda bi, i: (bi, 0, 1)),
        ],
        out_specs=pl.BlockSpec((None, tq, MEM_WIDTH), lambda bi, i: (bi, i, 0)),
        out_shape=jax.ShapeDtypeStruct((b, l, MEM_WIDTH), BF16),
        compiler_params=_params("parallel", "parallel"),
        name="mem_attn",
    )(proj, mem_kv, mem_kv)


def _rotate(x, cos2, sin2):
    return x * cos2 + pltpu.roll(x, HEAD_DIM // 2, axis=1) * sin2


def _retention_kernel(dec_ref, cos_ref, sin_ref, q_ref, k_ref, v_ref, g_ref, o_ref,
                      kr_ref, kv_ref, sf_ref, sb_ref, *, n_chunks):
    c = RET_CHUNK
    unroll = min(RET_UNROLL, n_chunks)
    log_gamma = jnp.log1p(-jnp.exp2(-dec_ref[...]))
    lgf, lgb = log_gamma[0:1, :], log_gamma[1:2, :]
    lane = lax.broadcasted_iota(jnp.int32, (1, c), 1).astype(F32)
    k_decay_f_row = jnp.exp((c - 1.0 - lane) * lgf)
    k_decay_b_row = jnp.exp(lane * lgb)
    chunk_decay_f = jnp.exp(c * lgf)
    chunk_decay_b = jnp.exp(c * lgb)

    def kv_step(i, carry):
        offs = [pl.multiple_of((i * unroll + u) * c, c) for u in range(unroll)]
        lhs = []
        for off in offs:
            kr = _rotate(k_ref[pl.ds(off, c), :].astype(F32), cos_ref[pl.ds(off, c), :],
                         sin_ref[pl.ds(off, c), :]) * ATTN_SCALE
            kr_ref[pl.ds(off, c), :] = kr.astype(BF16)
            kr_t = kr.T
            lhs.append(jnp.concatenate([kr_t * k_decay_f_row, kr_t * k_decay_b_row], axis=0).astype(BF16))
        for u, off in enumerate(offs):
            kv_ref[i * unroll + u] = jnp.dot(lhs[u], v_ref[pl.ds(off, c), :], preferred_element_type=F32)
        return carry

    lax.fori_loop(0, n_chunks // unroll, kv_step, 0)

    def scan_step(t, carry):
        sf, sb = carry
        nb = n_chunks - 1 - t
        sf_ref[t] = sf.astype(BF16)
        sb_ref[nb] = sb.astype(BF16)
        sf = chunk_decay_f * sf + kv_ref[t, 0:HEAD_DIM, :]
        sb = chunk_decay_b * sb + kv_ref[nb, HEAD_DIM:2 * HEAD_DIM, :]
        return sf, sb

    zero_state = jnp.zeros((HEAD_DIM, HEAD_DIM), F32)
    lax.fori_loop(0, n_chunks, scan_step, (zero_state, zero_state), unroll=unroll)

    row = lax.broadcasted_iota(jnp.int32, (c, c), 0).astype(F32)
    col = lax.broadcasted_iota(jnp.int32, (c, c), 1).astype(F32)
    diff = row - col
    decay_intra = jnp.exp(jnp.where(diff >= 0, diff * lgf, -diff * lgb))
    pos = lax.broadcasted_iota(jnp.int32, (c, HEAD_DIM), 0).astype(F32)
    q_decay_f = jnp.exp((pos + 1.0) * lgf)
    q_decay_b = jnp.exp((c - pos) * lgb)

    def out_step(i, carry):
        chunks = [i * unroll + u for u in range(unroll)]
        offs = [pl.multiple_of(n * c, c) for n in chunks]
        qrs, scores, ys = [], [], []
        for off in offs:
            qr = _rotate(q_ref[pl.ds(off, c), :].astype(F32), cos_ref[pl.ds(off, c), :],
                         sin_ref[pl.ds(off, c), :])
            qrs.append(qr)
            scores.append(lax.dot_general(qr.astype(BF16), kr_ref[pl.ds(off, c), :], NT_DIMS,
                                          preferred_element_type=F32))
        for n, off, qr, s in zip(chunks, offs, qrs, scores):
            lhs = jnp.concatenate([s * decay_intra, qr * q_decay_f, qr * q_decay_b], axis=1).astype(BF16)
            rhs = jnp.concatenate([v_ref[pl.ds(off, c), :], sf_ref[n], sb_ref[n]], axis=0)
            ys.append(jnp.dot(lhs, rhs, preferred_element_type=F32))
        for off, y in zip(offs, ys):
            y = y * lax.rsqrt(jnp.mean(y * y, axis=-1, keepdims=True) + NORM_EPS)
            gate = g_ref[pl.ds(off, c), :].astype(F32)
            o_ref[pl.ds(off, c), :] = (gate * (1.0 / (1.0 + jnp.exp(-gate))) * y).astype(o_ref.dtype)
        return carry

    lax.fori_loop(0, n_chunks // unroll, out_step, 0)


def _rotary_tables(l):
    half = HEAD_DIM // 2
    inv_freq = ROPE_BASE ** (-jnp.arange(half, dtype=F32) / half)
    ang = jnp.arange(l, dtype=F32)[:, None] * inv_freq[None, :]
    cos, sin = jnp.cos(ang), jnp.sin(ang)
    return jnp.concatenate([cos, cos], axis=-1), jnp.concatenate([-sin, sin], axis=-1)


def _retention(proj, decay_exp, tok_heads):
    b, l, _ = proj.shape
    assert l % RET_CHUNK == 0 and (l // RET_CHUNK) % min(RET_UNROLL, l // RET_CHUNK) == 0
    cos2, sin2 = _rotary_tables(l)
    dec = jnp.broadcast_to(decay_exp.astype(F32).T[:, :, None], (tok_heads, 2, HEAD_DIM))
    head_spec = lambda part: pl.BlockSpec((None, l, HEAD_DIM), lambda bi, h: (bi, 0, part * tok_heads + h))
    table_spec = pl.BlockSpec((l, HEAD_DIM), lambda bi, h: (0, 0), pipeline_mode=pl.Buffered(1))
    n_chunks = l // RET_CHUNK
    return pl.pallas_call(
        functools.partial(_retention_kernel, n_chunks=n_chunks),
        grid=(b, tok_heads),
        in_specs=[pl.BlockSpec((None, 2, HEAD_DIM), lambda bi, h: (h, 0, 0)), table_spec, table_spec,
                  head_spec(0), head_spec(1), head_spec(2), head_spec(3)],
        out_specs=pl.BlockSpec((None, l, HEAD_DIM), lambda bi, h: (bi, 0, h)),
        out_shape=jax.ShapeDtypeStruct((b, l, tok_heads * HEAD_DIM), BF16),
        scratch_shapes=[pltpu.VMEM((l, HEAD_DIM), BF16),
                        pltpu.VMEM((n_chunks, 2 * HEAD_DIM, HEAD_DIM), F32),
                        pltpu.VMEM((n_chunks, HEAD_DIM, HEAD_DIM), BF16),
                        pltpu.VMEM((n_chunks, HEAD_DIM, HEAD_DIM), BF16)],
        compiler_params=_params("parallel", "parallel"),
        name="retention",
    )(dec, cos2, sin2, proj, proj, proj, proj)


def _na_kernel(bias_ref, q_ref, k_ref, v_ref, o_ref, *, rows):
    win = NA_KH * GRID_W

    unroll = min(NA_UNROLL, rows)

    def rows_step(i, carry):
        qrows = [i * unroll + u for u in range(unroll)]
        starts = [jnp.clip(r - NA_KH // 2, 0, rows - NA_KH) for r in qrows]
        q_offs = [pl.multiple_of(r * GRID_W, GRID_W) for r in qrows]
        k_offs = [pl.multiple_of(rs * GRID_W, GRID_W) for rs in starts]
        scores = [lax.dot_general(q_ref[pl.ds(q_off, GRID_W), :], k_ref[pl.ds(k_off, win), :], NT_DIMS,
                                  preferred_element_type=F32)
                  for q_off, k_off in zip(q_offs, k_offs)]
        probs, denoms = [], []
        for r, rs, s in zip(qrows, starts, scores):
            s = s * (ATTN_SCALE * LOG2_E) + bias_ref[rs - r + NA_KH - 1]
            p = jnp.exp2(s - jnp.max(s, axis=-1, keepdims=True))
            denoms.append(jnp.sum(p, axis=-1, keepdims=True))
            probs.append(p.astype(BF16))
        outs = [jnp.dot(p, v_ref[pl.ds(k_off, win), :], preferred_element_type=F32)
                for p, k_off in zip(probs, k_offs)]
        for q_off, out, denom in zip(q_offs, outs, denoms):
            o_ref[pl.ds(q_off, GRID_W), :] = (out / denom).astype(o_ref.dtype)
        return carry

    lax.fori_loop(0, rows // unroll, rows_step, 0)


def _na_bias_table(rpb):
    c = np.arange(GRID_W)
    win_start = np.clip(c - NA_KW // 2, 0, GRID_W - NA_KW)
    valid = (c[None, :] >= win_start[:, None]) & (c[None, :] < win_start[:, None] + NA_KW)
    dc_idx = np.clip(c[None, :] - c[:, None] + NA_KW - 1, 0, 2 * NA_KW - 2)
    toeplitz = jnp.where(valid[None, None], rpb.astype(F32)[:, :, dc_idx] * LOG2_E, -jnp.inf)
    dr_idx = np.arange(NA_KH)[:, None] + np.arange(NA_KH)[None, :]
    table = toeplitz[:, dr_idx]
    table = table.transpose(0, 1, 3, 2, 4)
    return table.reshape(rpb.shape[0], NA_KH, GRID_W, NA_KH * GRID_W)


def _neighbourhood(proj, rpb, tok_heads):
    b, l, _ = proj.shape
    rows = l // GRID_W
    assert l % GRID_W == 0 and rows >= NA_KH and rows % min(NA_UNROLL, rows) == 0
    bias = _na_bias_table(rpb)
    head_spec = lambda part: pl.BlockSpec((None, l, HEAD_DIM), lambda bi, h: (bi, 0, part * tok_heads + h))
    return pl.pallas_call(
        functools.partial(_na_kernel, rows=rows),
        grid=(b, tok_heads),
        in_specs=[pl.BlockSpec((None, NA_KH, GRID_W, NA_KH * GRID_W), lambda bi, h: (h, 0, 0, 0)),
                  head_spec(0), head_spec(1), head_spec(2)],
        out_specs=pl.BlockSpec((None, l, HEAD_DIM), lambda bi, h: (bi, 0, h)),
        out_shape=jax.ShapeDtypeStruct((b, l, tok_heads * HEAD_DIM), BF16),
        compiler_params=_params("parallel", "parallel"),
        name="neighbourhood",
    )(bias, proj, proj, proj)


def _trunk(x, mem, norm_gain, mem_norm_gain, w_mem_kv, w_out, w_mlp_in, w_mlp_out, w_in_ret, ret_decay,
           w_in_na, na_rpb):
    b, l, d = x.shape
    n_mem = mem.shape[1]
    tok_width = w_out.shape[1] - MEM_WIDTH
    tok_heads = tok_width // HEAD_DIM
    xf = x.reshape(b * l, d)
    memf = mem.reshape(b * n_mem, d)
    for i in range(norm_gain.shape[0]):
        g = norm_gain[i]
        mem_kv = _norm_matmul(memf, mem_norm_gain[i], w_mem_kv, i).reshape(b, n_mem, 2 * MEM_WIDTH)
        j = i // N_MIXERS
        if i % N_MIXERS == 0:
            proj = _norm_matmul(xf, g[0], w_in_ret, j).reshape(b, l, -1)
            tok = _retention(proj, ret_decay[j], tok_heads)
        else:
            proj = _norm_matmul(xf, g[0], w_in_na, j).reshape(b, l, -1)
            tok = _neighbourhood(proj, na_rpb[j], tok_heads)
        mem_out = _mem_attn(proj, mem_kv)
        x1 = _out_proj(tok.reshape(b * l, tok_width), mem_out.reshape(b * l, MEM_WIDTH), xf, g[1], w_out, i)
        xf = _mlp(x1, g[2], g[3], w_mlp_in, w_mlp_out, i)
    return xf.reshape(b, l, d)


def kernel(x_prompt, x_sample, mem_prompt, mem_sample, norm_gain, mem_norm_gain, w_mem_kv, w_out, w_mlp_in,
           w_mlp_out, w_in_ret, ret_decay, w_in_na, na_rpb):
    weights = [w.astype(BF16) for w in (w_mem_kv, w_out, w_mlp_in, w_mlp_out, w_in_ret)]
    w_mem_kv, w_out, w_mlp_in, w_mlp_out, w_in_ret = weights
    w_mlp_in = _tile_columns(w_mlp_in, MLP_HIDDEN_TILE)
    w_in_na = w_in_na.astype(BF16)
    run = functools.partial(_trunk, norm_gain=norm_gain, mem_norm_gain=mem_norm_gain, w_mem_kv=w_mem_kv,
                            w_out=w_out, w_mlp_in=w_mlp_in, w_mlp_out=w_mlp_out, w_in_ret=w_in_ret,
                            ret_decay=ret_decay, w_in_na=w_in_na, na_rpb=na_rpb)
    return (run(x_prompt, mem_prompt), run(x_sample, mem_sample))
```

```python
import functools

import jax
import jax.numpy as jnp
import numpy as np
from jax import lax
from jax.experimental import pallas as pl
from jax.experimental.pallas import tpu as pltpu

HEAD_DIM = 128
MEM_HEADS = 4
MEM_WIDTH = MEM_HEADS * HEAD_DIM
N_MIXERS = 2
RET_CHUNK = 128
ROPE_BASE = 10000.0
GRID_W = 64
NA_KH = 8
NA_KW = 16
NORM_EPS = 1e-6
ATTN_SCALE = HEAD_DIM ** -0.5
LOG2_E = 1.4426950408889634
MLP_HIDDEN_TILE = 1024
RET_UNROLL = 16
NA_UNROLL = 16

V7X_VMEM_BYTES = 64 * 1024 * 1024
VMEM_LIMIT_BYTES = V7X_VMEM_BYTES - 4 * 1024 * 1024

F32 = jnp.float32
BF16 = jnp.bfloat16
NT_DIMS = (((1,), (1,)), ((), ()))


def _params(*semantics):
    return pltpu.CompilerParams(dimension_semantics=semantics, vmem_limit_bytes=VMEM_LIMIT_BYTES)


def _rms(x, gain):
    return x * lax.rsqrt(jnp.mean(x * x, axis=-1, keepdims=True) + NORM_EPS) * gain


def _norm_matmul_kernel(x_ref, g_ref, w_ref, o_ref, *, sub_rows):
    gain = g_ref[...]
    for r in range(x_ref.shape[0] // sub_rows):
        rows = slice(r * sub_rows, (r + 1) * sub_rows)
        h = _rms(x_ref[rows, :], gain).astype(BF16)
        o_ref[rows, :] = jnp.dot(h, w_ref[...], preferred_element_type=F32).astype(o_ref.dtype)


def _norm_matmul(x, gain, w, layer, *, tm=512, sub_rows=256):
    t, d = x.shape
    n = w.shape[2]
    tm = min(tm, t)
    sub_rows = min(sub_rows, tm)
    assert t % tm == 0 and tm % sub_rows == 0
    return pl.pallas_call(
        functools.partial(_norm_matmul_kernel, sub_rows=sub_rows),
        grid=(t // tm,),
        in_specs=[
            pl.BlockSpec((tm, d), lambda i: (i, 0)),
            pl.BlockSpec((1, d), lambda i: (0, 0)),
            pl.BlockSpec((None, d, n), lambda i: (layer, 0, 0), pipeline_mode=pl.Buffered(1)),
        ],
        out_specs=pl.BlockSpec((tm, n), lambda i: (i, 0)),
        out_shape=jax.ShapeDtypeStruct((t, n), BF16),
        compiler_params=_params("parallel"),
        name="norm_matmul",
    )(x, gain.reshape(1, d), w)


def _out_proj_kernel(tok_ref, mem_ref, x_ref, g_ref, wt_ref, wm_ref, o_ref, *, sub_rows):
    gain = g_ref[...]
    for r in range(x_ref.shape[0] // sub_rows):
        rows = slice(r * sub_rows, (r + 1) * sub_rows)
        mixed = (jnp.dot(tok_ref[rows, :], wt_ref[...], preferred_element_type=F32)
                 + jnp.dot(mem_ref[rows, :], wm_ref[...], preferred_element_type=F32))
        o_ref[rows, :] = x_ref[rows, :] + _rms(mixed, gain)


def _out_proj(tok, mem_out, x, gain, w_out, layer, *, tm=1024, sub_rows=256):
    t, d = x.shape
    tm = min(tm, t)
    sub_rows = min(sub_rows, tm)
    tw, mw = tok.shape[1], mem_out.shape[1]
    assert t % tm == 0 and tm % sub_rows == 0 and tw % mw == 0 and w_out.shape[1] == tw + mw
    return pl.pallas_call(
        functools.partial(_out_proj_kernel, sub_rows=sub_rows),
        grid=(t // tm,),
        in_specs=[
            pl.BlockSpec((tm, tw), lambda i: (i, 0)),
            pl.BlockSpec((tm, mw), lambda i: (i, 0)),
            pl.BlockSpec((tm, d), lambda i: (i, 0)),
            pl.BlockSpec((1, d), lambda i: (0, 0)),
            pl.BlockSpec((None, tw, d), lambda i: (layer, 0, 0), pipeline_mode=pl.Buffered(1)),
            pl.BlockSpec((None, mw, d), lambda i: (layer, tw // mw, 0), pipeline_mode=pl.Buffered(1)),
        ],
        out_specs=pl.BlockSpec((tm, d), lambda i: (i, 0)),
        out_shape=jax.ShapeDtypeStruct((t, d), F32),
        compiler_params=_params("parallel"),
        name="out_proj",
    )(tok, mem_out, x, gain.reshape(1, d), w_out, w_out)


def _mlp_kernel(x_ref, gi_ref, go_ref, w1_ref, w2_ref, o_ref, h_ref, *, sub_rows):
    j = pl.program_id(1)
    last = pl.num_programs(1) - 1
    n_sub = x_ref.shape[0] // sub_rows

    def step(first, final):
        hidden = []
        for r in range(n_sub):
            rows = slice(r * sub_rows, (r + 1) * sub_rows)
            if first:
                h = _rms(x_ref[rows, :], gi_ref[...]).astype(BF16)
                h_ref[rows, :] = h
            else:
                h = h_ref[rows, :]
            u = jnp.dot(h, w1_ref[...], preferred_element_type=F32)
            hidden.append(jnp.square(jnp.maximum(u, 0.0)).astype(BF16))
        for r in range(n_sub):
            rows = slice(r * sub_rows, (r + 1) * sub_rows)
            acc = jnp.dot(hidden[r], w2_ref[...], preferred_element_type=F32)
            if not first:
                acc = o_ref[rows, :] + acc
            if final:
                acc = x_ref[rows, :] + _rms(acc, go_ref[...])
            o_ref[rows, :] = acc

    pl.when(j == 0)(lambda: step(True, False))
    pl.when(jnp.logical_and(j > 0, j < last))(lambda: step(False, False))
    pl.when(j == last)(lambda: step(False, True))


def _mlp(x, gain_in, gain_out, w1, w2, layer, *, tm=1024, tf=MLP_HIDDEN_TILE, sub_rows=512):
    t, d = x.shape
    f = w1.shape[2]
    tm = min(tm, t)
    sub_rows = min(sub_rows, tm)
    assert t % tm == 0 and tm % sub_rows == 0 and f % tf == 0 and f // tf >= 2
    return pl.pallas_call(
        functools.partial(_mlp_kernel, sub_rows=sub_rows),
        grid=(t // tm, f // tf),
        in_specs=[
            pl.BlockSpec((tm, d), lambda i, j: (i, 0)),
            pl.BlockSpec((1, d), lambda i, j: (0, 0)),
            pl.BlockSpec((1, d), lambda i, j: (0, 0)),
            pl.BlockSpec((None, d, tf), lambda i, j: (layer, 0, j)),
            pl.BlockSpec((None, tf, d), lambda i, j: (layer, j, 0)),
        ],
        out_specs=pl.BlockSpec((tm, d), lambda i, j: (i, 0)),
        out_shape=jax.ShapeDtypeStruct((t, d), F32),
        scratch_shapes=[pltpu.VMEM((tm, d), BF16)],
        compiler_params=_params("parallel", "arbitrary"),
        name="mlp",
    )(x, gain_in.reshape(1, d), gain_out.reshape(1, d), w1, w2)


def _mem_attn_kernel(q_ref, k_ref, v_ref, o_ref):
    heads = [slice(h * HEAD_DIM, (h + 1) * HEAD_DIM) for h in range(MEM_HEADS)]
    scores = [lax.dot_general(q_ref[:, cols], k_ref[:, cols], NT_DIMS, preferred_element_type=F32)
              for cols in heads]
    probs, denoms = [], []
    for s in scores:
        s = s * ATTN_SCALE
        p = jnp.exp(s - jnp.max(s, axis=-1, keepdims=True))
        denoms.append(jnp.sum(p, axis=-1, keepdims=True))
        probs.append(p.astype(BF16))
    outs = [jnp.dot(p, v_ref[:, cols], preferred_element_type=F32) for p, cols in zip(probs, heads)]
    for cols, out, denom in zip(heads, outs, denoms):
        o_ref[:, cols] = (out / denom).astype(o_ref.dtype)


def _mem_attn(proj, mem_kv, *, tq=1024):
    b, l, n = proj.shape
    m = mem_kv.shape[1]
    tq = min(tq, l)
    assert l % tq == 0 and n % MEM_WIDTH == 0
    q_block = n // MEM_WIDTH - 1
    return pl.pallas_call(
        _mem_attn_kernel,
        grid=(b, l // tq),
        in_specs=[
            pl.BlockSpec((None, tq, MEM_WIDTH), lambda bi, i: (bi, i, q_block)),
            pl.BlockSpec((None, m, MEM_WIDTH), lambda bi, i: (bi, 0, 0)),
            pl.BlockSpec((None, m, MEM_WIDTH), lambda bi, i: (bi, 0, 1)),
        ],
        out_specs=pl.BlockSpec((None, tq, MEM_WIDTH), lambda bi, i: (bi, i, 0)),
        out_shape=jax.ShapeDtypeStruct((b, l, MEM_WIDTH), BF16),
        compiler_params=_params("parallel", "parallel"),
        name="mem_attn",
    )(proj, mem_kv, mem_kv)


def _rotate(x, cos2, sin2):
    return x * cos2 + pltpu.roll(x, HEAD_DIM // 2, axis=1) * sin2


def _retention_kernel(dec_ref, cos_ref, sin_ref, q_ref, k_ref, v_ref, g_ref, o_ref,
                      kr_ref, kv_ref, sf_ref, sb_ref, *, n_chunks):
    c = RET_CHUNK
    unroll = min(RET_UNROLL, n_chunks)
    log_gamma = jnp.log1p(-jnp.exp2(-dec_ref[...]))
    lgf, lgb = log_gamma[0:1, :], log_gamma[1:2, :]
    lane = lax.broadcasted_iota(jnp.int32, (1, c), 1).astype(F32)
    k_decay_f_row = jnp.exp((c - 1.0 - lane) * lgf)
    k_decay_b_row = jnp.exp(lane * lgb)
    chunk_decay_f = jnp.exp(c * lgf)
    chunk_decay_b = jnp.exp(c * lgb)

    def kv_step(i, carry):
        offs = [pl.multiple_of((i * unroll + u) * c, c) for u in range(unroll)]
        lhs = []
        for off in offs:
            kr = _rotate(k_ref[pl.ds(off, c), :].astype(F32), cos_ref[pl.ds(off, c), :],
                         sin_ref[pl.ds(off, c), :]) * ATTN_SCALE
            kr_ref[pl.ds(off, c), :] = kr.astype(BF16)
            kr_t = kr.T
            lhs.append(jnp.concatenate([kr_t * k_decay_f_row, kr_t * k_decay_b_row], axis=0).astype(BF16))
        for u, off in enumerate(offs):
            kv_ref[i * unroll + u] = jnp.dot(lhs[u], v_ref[pl.ds(off, c), :], preferred_element_type=F32)
        return carry

    lax.fori_loop(0, n_chunks // unroll, kv_step, 0)

    def scan_step(t, carry):
        sf, sb = carry
        nb = n_chunks - 1 - t
        sf_ref[t] = sf.astype(BF16)
        sb_ref[nb] = sb.astype(BF16)
        sf = chunk_decay_f * sf + kv_ref[t, 0:HEAD_DIM, :]
        sb = chunk_decay_b * sb + kv_ref[nb, HEAD_DIM:2 * HEAD_DIM, :]
        return sf, sb

    zero_state = jnp.zeros((HEAD_DIM, HEAD_DIM), F32)
    lax.fori_loop(0, n_chunks, scan_step, (zero_state, zero_state), unroll=unroll)

    row = lax.broadcasted_iota(jnp.int32, (c, c), 0).astype(F32)
    col = lax.broadcasted_iota(jnp.int32, (c, c), 1).astype(F32)
    diff = row - col
    decay_intra = jnp.exp(jnp.where(diff >= 0, diff * lgf, -diff * lgb))
    pos = lax.broadcasted_iota(jnp.int32, (c, HEAD_DIM), 0).astype(F32)
    q_decay_f = jnp.exp((pos + 1.0) * lgf)
    q_decay_b = jnp.exp((c - pos) * lgb)

    def out_step(i, carry):
        chunks = [i * unroll + u for u in range(unroll)]
        offs = [pl.multiple_of(n * c, c) for n in chunks]
        qrs, scores, ys = [], [], []
        for off in offs:
            qr = _rotate(q_ref[pl.ds(off, c), :].astype(F32), cos_ref[pl.ds(off, c), :],
                         sin_ref[pl.ds(off, c), :])
            qrs.append(qr)
            scores.append(lax.dot_general(qr.astype(BF16), kr_ref[pl.ds(off, c), :], NT_DIMS,
                                          preferred_element_type=F32))
        for n, off, qr, s in zip(chunks, offs, qrs, scores):
            lhs = jnp.concatenate([s * decay_intra, qr * q_decay_f, qr * q_decay_b], axis=1).astype(BF16)
            rhs = jnp.concatenate([v_ref[pl.ds(off, c), :], sf_ref[n], sb_ref[n]], axis=0)
            ys.append(jnp.dot(lhs, rhs, preferred_element_type=F32))
        for off, y in zip(offs, ys):
            y = y * lax.rsqrt(jnp.mean(y * y, axis=-1, keepdims=True) + NORM_EPS)
            gate = g_ref[pl.ds(off, c), :].astype(F32)
            o_ref[pl.ds(off, c), :] = (gate * (1.0 / (1.0 + jnp.exp(-gate))) * y).astype(o_ref.dtype)
        return carry

    lax.fori_loop(0, n_chunks // unroll, out_step, 0)


def _rotary_tables(l):
    half = HEAD_DIM // 2
    inv_freq = ROPE_BASE ** (-jnp.arange(half, dtype=F32) / half)
    ang = jnp.arange(l, dtype=F32)[:, None] * inv_freq[None, :]
    cos, sin = jnp.cos(ang), jnp.sin(ang)
    return jnp.concatenate([cos, cos], axis=-1), jnp.concatenate([-sin, sin], axis=-1)


def _retention(proj, decay_exp, tok_heads):
    b, l, _ = proj.shape
    assert l % RET_CHUNK == 0 and (l // RET_CHUNK) % min(RET_UNROLL, l // RET_CHUNK) == 0
    cos2, sin2 = _rotary_tables(l)
    dec = jnp.broadcast_to(decay_exp.astype(F32).T[:, :, None], (tok_heads, 2, HEAD_DIM))
    head_spec = lambda part: pl.BlockSpec((None, l, HEAD_DIM), lambda bi, h: (bi, 0, part * tok_heads + h))
    table_spec = pl.BlockSpec((l, HEAD_DIM), lambda bi, h: (0, 0), pipeline_mode=pl.Buffered(1))
    n_chunks = l // RET_CHUNK
    return pl.pallas_call(
        functools.partial(_retention_kernel, n_chunks=n_chunks),
        grid=(b, tok_heads),
        in_specs=[pl.BlockSpec((None, 2, HEAD_DIM), lambda bi, h: (h, 0, 0)), table_spec, table_spec,
                  head_spec(0), head_spec(1), head_spec(2), head_spec(3)],
        out_specs=pl.BlockSpec((None, l, HEAD_DIM), lambda bi, h: (bi, 0, h)),
        out_shape=jax.ShapeDtypeStruct((b, l, tok_heads * HEAD_DIM), BF16),
        scratch_shapes=[pltpu.VMEM((l, HEAD_DIM), BF16),
                        pltpu.VMEM((n_chunks, 2 * HEAD_DIM, HEAD_DIM), F32),
                        pltpu.VMEM((n_chunks, HEAD_DIM, HEAD_DIM), BF16),
                        pltpu.VMEM((n_chunks, HEAD_DIM, HEAD_DIM), BF16)],
        compiler_params=_params("parallel", "parallel"),
        name="retention",
    )(dec, cos2, sin2, proj, proj, proj, proj)


def _na_kernel(bias_ref, q_ref, k_ref, v_ref, o_ref, *, rows):
    win = NA_KH * GRID_W

    unroll = min(NA_UNROLL, rows)

    def rows_step(i, carry):
        qrows = [i * unroll + u for u in range(unroll)]
        starts = [jnp.clip(r - NA_KH // 2, 0, rows - NA_KH) for r in qrows]
        q_offs = [pl.multiple_of(r * GRID_W, GRID_W) for r in qrows]
        k_offs = [pl.multiple_of(rs * GRID_W, GRID_W) for rs in starts]
        scores = [lax.dot_general(q_ref[pl.ds(q_off, GRID_W), :], k_ref[pl.ds(k_off, win), :], NT_DIMS,
                                  preferred_element_type=F32)
                  for q_off, k_off in zip(q_offs, k_offs)]
        probs, denoms = [], []
        for r, rs, s in zip(qrows, starts, scores):
            s = s * (ATTN_SCALE * LOG2_E) + bias_ref[rs - r + NA_KH - 1]
            p = jnp.exp2(s - jnp.max(s, axis=-1, keepdims=True))
            denoms.append(jnp.sum(p, axis=-1, keepdims=True))
            probs.append(p.astype(BF16))
        outs = [jnp.dot(p, v_ref[pl.ds(k_off, win), :], preferred_element_type=F32)
                for p, k_off in zip(probs, k_offs)]
        for q_off, out, denom in zip(q_offs, outs, denoms):
            o_ref[pl.ds(q_off, GRID_W), :] = (out / denom).astype(o_ref.dtype)
        return carry

    lax.fori_loop(0, rows // unroll, rows_step, 0)


def _na_bias_table(rpb):
    c = np.arange(GRID_W)
    win_start = np.clip(c - NA_KW // 2, 0, GRID_W - NA_KW)
    valid = (c[None, :] >= win_start[:, None]) & (c[None, :] < win_start[:, None] + NA_KW)
    dc_idx = np.clip(c[None, :] - c[:, None] + NA_KW - 1, 0, 2 * NA_KW - 2)
    toeplitz = jnp.where(valid[None, None], rpb.astype(F32)[:, :, dc_idx] * LOG2_E, -jnp.inf)
    dr_idx = np.arange(NA_KH)[:, None] + np.arange(NA_KH)[None, :]
    table = toeplitz[:, dr_idx]
    table = table.transpose(0, 1, 3, 2, 4)
    return table.reshape(rpb.shape[0], NA_KH, GRID_W, NA_KH * GRID_W)


def _neighbourhood(proj, rpb, tok_heads):
    b, l, _ = proj.shape
    rows = l // GRID_W
    assert l % GRID_W == 0 and rows >= NA_KH and rows % min(NA_UNROLL, rows) == 0
    bias = _na_bias_table(rpb)
    head_spec = lambda part: pl.BlockSpec((None, l, HEAD_DIM), lambda bi, h: (bi, 0, part * tok_heads + h))
    return pl.pallas_call(
        functools.partial(_na_kernel, rows=rows),
        grid=(b, tok_heads),
        in_specs=[pl.BlockSpec((None, NA_KH, GRID_W, NA_KH * GRID_W), lambda bi, h: (h, 0, 0, 0)),
                  head_spec(0), head_spec(1), head_spec(2)],
        out_specs=pl.BlockSpec((None, l, HEAD_DIM), lambda bi, h: (bi, 0, h)),
        out_shape=jax.ShapeDtypeStruct((b, l, tok_heads * HEAD_DIM), BF16),
        compiler_params=_params("parallel", "parallel"),
        name="neighbourhood",
    )(bias, proj, proj, proj)


def _trunk(x, mem, norm_gain, mem_norm_gain, w_mem_kv, w_out, w_mlp_in, w_mlp_out, w_in_ret, ret_decay,
           w_in_na, na_rpb):
    b, l, d = x.shape
    n_mem = mem.shape[1]
    tok_width = w_out.shape[1] - MEM_WIDTH
    tok_heads = tok_width // HEAD_DIM
    xf = x.reshape(b * l, d)
    memf = mem.reshape(b * n_mem, d)
    for i in range(norm_gain.shape[0]):
        g = norm_gain[i]
        mem_kv = _norm_matmul(memf, mem_norm_gain[i], w_mem_kv, i).reshape(b, n_mem, 2 * MEM_WIDTH)
        j = i // N_MIXERS
        if i % N_MIXERS == 0:
            proj = _norm_matmul(xf, g[0], w_in_ret, j).reshape(b, l, -1)
            tok = _retention(proj, ret_decay[j], tok_heads)
        else:
            proj = _norm_matmul(xf, g[0], w_in_na, j).reshape(b, l, -1)
            tok = _neighbourhood(proj, na_rpb[j], tok_heads)
        mem_out = _mem_attn(proj, mem_kv)
        x1 = _out_proj(tok.reshape(b * l, tok_width), mem_out.reshape(b * l, MEM_WIDTH), xf, g[1], w_out, i)
        xf = _mlp(x1, g[2], g[3], w_mlp_in, w_mlp_out, i)
    return xf.reshape(b, l, d)


def kernel(x_prompt, x_sample, mem_prompt, mem_sample, norm_gain, mem_norm_gain, w_mem_kv, w_out, w_mlp_in,
           w_mlp_out, w_in_ret, ret_decay, w_in_na, na_rpb):
    weights = [w.astype(BF16) for w in (w_mem_kv, w_out, w_mlp_in, w_mlp_out, w_in_ret)]
    w_mem_kv, w_out, w_mlp_in, w_mlp_out, w_in_ret = weights
    w_in_na = w_in_na.astype(BF16)
    run = functools.partial(_trunk, norm_gain=norm_gain, mem_norm_gain=mem_norm_gain, w_mem_kv=w_mem_kv,
                            w_out=w_out, w_mlp_in=w_mlp_in, w_mlp_out=w_mlp_out, w_in_ret=w_in_ret,
                            ret_decay=ret_decay, w_in_na=w_in_na, na_rpb=na_rpb)
    return (run(x_prompt, mem_prompt), run(x_sample, mem_sample))
```

```python
import functools

import jax
import jax.numpy as jnp
import numpy as np
from jax import lax
from jax.experimental import pallas as pl
from jax.experimental.pallas import tpu as pltpu

HEAD_DIM = 128
MEM_HEADS = 4
MEM_WIDTH = MEM_HEADS * HEAD_DIM
N_MIXERS = 2
RET_CHUNK = 128
ROPE_BASE = 10000.0
GRID_W = 64
NA_KH = 8
NA_KW = 16
NORM_EPS = 1e-6
ATTN_SCALE = HEAD_DIM ** -0.5
LOG2_E = 1.4426950408889634
MLP_HIDDEN_TILE = 1024
RET_UNROLL = 16
NA_UNROLL = 16

V7X_VMEM_BYTES = 64 * 1024 * 1024
VMEM_LIMIT_BYTES = V7X_VMEM_BYTES - 4 * 1024 * 1024

F32 = jnp.float32
BF16 = jnp.bfloat16
NT_DIMS = (((1,), (1,)), ((), ()))


def _params(*semantics):
    return pltpu.CompilerParams(dimension_semantics=semantics, vmem_limit_bytes=VMEM_LIMIT_BYTES)


def _rms(x, gain):
    return x * lax.rsqrt(jnp.mean(x * x, axis=-1, keepdims=True) + NORM_EPS) * gain


def _rotate(x, cos2, sin2):
    return x * cos2 + pltpu.roll(x, HEAD_DIM // 2, axis=1) * sin2


def _retention_columns(res, cos2, sin2, tok_width):
    heads = tok_width // HEAD_DIM
    blocks = []
    for c in range(res.shape[1] // HEAD_DIM):
        blk = res[:, c * HEAD_DIM:(c + 1) * HEAD_DIM]
        part = c // heads
        if part == 0:
            blk = _rotate(blk, cos2, sin2)
        elif part == 1:
            blk = _rotate(blk, cos2, sin2) * ATTN_SCALE
        elif part == 3:
            blk = blk * (1.0 / (1.0 + jnp.exp(-blk)))
        blocks.append(blk.astype(BF16))
    return jnp.concatenate(blocks, axis=1)


def _neighbourhood_columns(res, tok_width):
    q = res[:, :tok_width] * (ATTN_SCALE * LOG2_E)
    return jnp.concatenate([q.astype(BF16), res[:, tok_width:].astype(BF16)], axis=1)


def _norm_matmul_kernel(x_ref, g_ref, w_ref, *rest, sub_rows, columns, tok_width):
    o_ref = rest[-1]
    gain = g_ref[...]
    for r in range(x_ref.shape[0] // sub_rows):
        rows = slice(r * sub_rows, (r + 1) * sub_rows)
        h = _rms(x_ref[rows, :], gain).astype(BF16)
        res = jnp.dot(h, w_ref[...], preferred_element_type=F32)
        if columns == "retention":
            cos_ref, sin_ref = rest[0], rest[1]
            res = _retention_columns(res, cos_ref[rows, :], sin_ref[rows, :], tok_width)
        elif columns == "neighbourhood":
            res = _neighbourhood_columns(res, tok_width)
        o_ref[rows, :] = res.astype(o_ref.dtype)


def _norm_matmul(x, gain, w, layer, *, columns=None, tok_width=None, seq=None, tm=512, sub_rows=256):
    t, d = x.shape
    n = w.shape[2]
    tm = min(tm, t if seq is None else seq)
    sub_rows = min(sub_rows, tm)
    assert t % tm == 0 and tm % sub_rows == 0
    operands = [x, gain.reshape(1, d), w]
    in_specs = [
        pl.BlockSpec((tm, d), lambda i: (i, 0)),
        pl.BlockSpec((1, d), lambda i: (0, 0)),
        pl.BlockSpec((None, d, n), lambda i: (layer, 0, 0), pipeline_mode=pl.Buffered(1)),
    ]
    if columns == "retention":
        assert seq % tm == 0
        tiles_per_seq = seq // tm
        operands += list(_rotary_tables(seq))
        in_specs += [pl.BlockSpec((tm, HEAD_DIM), lambda i: (i % tiles_per_seq, 0)),
                     pl.BlockSpec((tm, HEAD_DIM), lambda i: (i % tiles_per_seq, 0))]
    return pl.pallas_call(
        functools.partial(_norm_matmul_kernel, sub_rows=sub_rows, columns=columns, tok_width=tok_width),
        grid=(t // tm,),
        in_specs=in_specs,
        out_specs=pl.BlockSpec((tm, n), lambda i: (i, 0)),
        out_shape=jax.ShapeDtypeStruct((t, n), BF16),
        compiler_params=_params("parallel"),
        name="norm_matmul",
    )(*operands)


def _mem_scores(q, k_ref):
    return [lax.dot_general(q[:, h * HEAD_DIM:(h + 1) * HEAD_DIM], k_ref[:, h * HEAD_DIM:(h + 1) * HEAD_DIM],
                            NT_DIMS, preferred_element_type=F32) for h in range(MEM_HEADS)]


def _mem_values(scores, v_ref):
    probs, denoms = [], []
    for s in scores:
        s = s * ATTN_SCALE
        p = jnp.exp(s - jnp.max(s, axis=-1, keepdims=True))
        denoms.append(jnp.sum(p, axis=-1, keepdims=True))
        probs.append(p.astype(BF16))
    outs = [jnp.dot(p, v_ref[:, h * HEAD_DIM:(h + 1) * HEAD_DIM], preferred_element_type=F32)
            for h, p in enumerate(probs)]
    return jnp.concatenate([(o / dn).astype(BF16) for o, dn in zip(outs, denoms)], axis=1)


def _out_proj_kernel(tok_ref, qm_ref, k_ref, v_ref, x_ref, g_ref, wt_ref, wm_ref, o_ref, *, sub_rows):
    gain = g_ref[...]
    n_sub = x_ref.shape[0] // sub_rows
    rows = [slice(r * sub_rows, (r + 1) * sub_rows) for r in range(n_sub)]
    mem = _mem_values(_mem_scores(qm_ref[rows[0], :], k_ref), v_ref)
    for r in range(n_sub):
        if r + 1 < n_sub:
            next_scores = _mem_scores(qm_ref[rows[r + 1], :], k_ref)
        mixed = (jnp.dot(tok_ref[rows[r], :], wt_ref[...], preferred_element_type=F32)
                 + jnp.dot(mem, wm_ref[...], preferred_element_type=F32))
        if r + 1 < n_sub:
            mem = _mem_values(next_scores, v_ref)
        o_ref[rows[r], :] = x_ref[rows[r], :] + _rms(mixed, gain)


def _out_proj(tok, proj, mem_kv, x, gain, w_out, layer, *, tm=1024, sub_rows=256):
    t, d = x.shape
    b, m, _ = mem_kv.shape
    seq = t // b
    tm = min(tm, seq)
    sub_rows = min(sub_rows, tm)
    tw, n = tok.shape[1], proj.shape[1]
    assert seq % tm == 0 and tm % sub_rows == 0 and tw % MEM_WIDTH == 0 and n % MEM_WIDTH == 0
    assert w_out.shape[1] == tw + MEM_WIDTH
    tiles_per_seq = seq // tm
    q_block = n // MEM_WIDTH - 1
    return pl.pallas_call(
        functools.partial(_out_proj_kernel, sub_rows=sub_rows),
        grid=(t // tm,),
        in_specs=[
            pl.BlockSpec((tm, tw), lambda i: (i, 0)),
            pl.BlockSpec((tm, MEM_WIDTH), lambda i: (i, q_block)),
            pl.BlockSpec((None, m, MEM_WIDTH), lambda i: (i // tiles_per_seq, 0, 0)),
            pl.BlockSpec((None, m, MEM_WIDTH), lambda i: (i // tiles_per_seq, 0, 1)),
            pl.BlockSpec((tm, d), lambda i: (i, 0)),
            pl.BlockSpec((1, d), lambda i: (0, 0)),
            pl.BlockSpec((None, tw, d), lambda i: (layer, 0, 0), pipeline_mode=pl.Buffered(1)),
            pl.BlockSpec((None, MEM_WIDTH, d), lambda i: (layer, tw // MEM_WIDTH, 0),
                         pipeline_mode=pl.Buffered(1)),
        ],
        out_specs=pl.BlockSpec((tm, d), lambda i: (i, 0)),
        out_shape=jax.ShapeDtypeStruct((t, d), F32),
        compiler_params=_params("parallel"),
        name="out_proj",
    )(tok, proj, mem_kv, mem_kv, x, gain.reshape(1, d), w_out, w_out)


def _mlp_kernel(x_ref, gi_ref, go_ref, w1_ref, w2_ref, o_ref, h_ref, *, sub_rows):
    j = pl.program_id(1)
    last = pl.num_programs(1) - 1
    n_sub = x_ref.shape[0] // sub_rows

    def step(first, final):
        hidden = []
        for r in range(n_sub):
            rows = slice(r * sub_rows, (r + 1) * sub_rows)
            if first:
                h = _rms(x_ref[rows, :], gi_ref[...]).astype(BF16)
                h_ref[rows, :] = h
            else:
                h = h_ref[rows, :]
            u = jnp.dot(h, w1_ref[...], preferred_element_type=F32)
            hidden.append(jnp.square(jnp.maximum(u, 0.0)).astype(BF16))
        for r in range(n_sub):
            rows = slice(r * sub_rows, (r + 1) * sub_rows)
            acc = jnp.dot(hidden[r], w2_ref[...], preferred_element_type=F32)
            if not first:
                acc = o_ref[rows, :] + acc
            if final:
                acc = x_ref[rows, :] + _rms(acc, go_ref[...])
            o_ref[rows, :] = acc

    pl.when(j == 0)(lambda: step(True, False))
    pl.when(jnp.logical_and(j > 0, j < last))(lambda: step(False, False))
    pl.when(j == last)(lambda: step(False, True))


def _mlp(x, gain_in, gain_out, w1, w2, layer, *, tm=1024, tf=MLP_HIDDEN_TILE, sub_rows=512):
    t, d = x.shape
    f = w1.shape[2]
    tm = min(tm, t)
    sub_rows = min(sub_rows, tm)
    assert t % tm == 0 and tm % sub_rows == 0 and f % tf == 0 and f // tf >= 2
    return pl.pallas_call(
        functools.partial(_mlp_kernel, sub_rows=sub_rows),
        grid=(t // tm, f // tf),
        in_specs=[
            pl.BlockSpec((tm, d), lambda i, j: (i, 0)),
            pl.BlockSpec((1, d), lambda i, j: (0, 0)),
            pl.BlockSpec((1, d), lambda i, j: (0, 0)),
            pl.BlockSpec((None, d, tf), lambda i, j: (layer, 0, j)),
            pl.BlockSpec((None, tf, d), lambda i, j: (layer, j, 0)),
        ],
        out_specs=pl.BlockSpec((tm, d), lambda i, j: (i, 0)),
        out_shape=jax.ShapeDtypeStruct((t, d), F32),
        scratch_shapes=[pltpu.VMEM((tm, d), BF16)],
        compiler_params=_params("parallel", "arbitrary"),
        name="mlp",
    )(x, gain_in.reshape(1, d), gain_out.reshape(1, d), w1, w2)


def _retention_kernel(dec_ref, q_ref, k_ref, v_ref, g_ref, o_ref, kv_ref, sf_ref, sb_ref, *, n_chunks):
    c = RET_CHUNK
    unroll = min(RET_UNROLL, n_chunks)
    log_gamma = jnp.log1p(-jnp.exp2(-dec_ref[...]))
    lgf, lgb = log_gamma[0:1, :], log_gamma[1:2, :]
    lane = lax.broadcasted_iota(jnp.int32, (1, c), 1).astype(F32)
    k_decay_f_row = jnp.exp((c - 1.0 - lane) * lgf)
    k_decay_b_row = jnp.exp(lane * lgb)
    chunk_decay_f = jnp.exp(c * lgf)
    chunk_decay_b = jnp.exp(c * lgb)

    def kv_step(i, carry):
        offs = [pl.multiple_of((i * unroll + u) * c, c) for u in range(unroll)]
        lhs = []
        for off in offs:
            kr_t = k_ref[pl.ds(off, c), :].astype(F32).T
            lhs.append(jnp.concatenate([kr_t * k_decay_f_row, kr_t * k_decay_b_row], axis=0).astype(BF16))
        for u, off in enumerate(offs):
            kv_ref[i * unroll + u] = jnp.dot(lhs[u], v_ref[pl.ds(off, c), :], preferred_element_type=F32)
        return carry

    lax.fori_loop(0, n_chunks // unroll, kv_step, 0)

    def scan_step(t, carry):
        sf, sb = carry
        nb = n_chunks - 1 - t
        sf_ref[t] = sf.astype(BF16)
        sb_ref[nb] = sb.astype(BF16)
        sf = chunk_decay_f * sf + kv_ref[t, 0:HEAD_DIM, :]
        sb = chunk_decay_b * sb + kv_ref[nb, HEAD_DIM:2 * HEAD_DIM, :]
        return sf, sb

    zero_state = jnp.zeros((HEAD_DIM, HEAD_DIM), F32)
    lax.fori_loop(0, n_chunks, scan_step, (zero_state, zero_state), unroll=unroll)

    row = lax.broadcasted_iota(jnp.int32, (c, c), 0).astype(F32)
    col = lax.broadcasted_iota(jnp.int32, (c, c), 1).astype(F32)
    diff = row - col
    decay_intra = jnp.exp(jnp.where(diff >= 0, diff * lgf, -diff * lgb))
    pos = lax.broadcasted_iota(jnp.int32, (c, HEAD_DIM), 0).astype(F32)
    q_decay_f = jnp.exp((pos + 1.0) * lgf)
    q_decay_b = jnp.exp((c - pos) * lgb)

    def out_step(i, carry):
        chunks = [i * unroll + u for u in range(unroll)]
        offs = [pl.multiple_of(n * c, c) for n in chunks]
        scores, ys = [], []
        for off in offs:
            scores.append(lax.dot_general(q_ref[pl.ds(off, c), :], k_ref[pl.ds(off, c), :], NT_DIMS,
                                          preferred_element_type=F32))
        for n, off, s in zip(chunks, offs, scores):
            qr = q_ref[pl.ds(off, c), :].astype(F32)
            lhs = jnp.concatenate([s * decay_intra, qr * q_decay_f, qr * q_decay_b], axis=1).astype(BF16)
            rhs = jnp.concatenate([v_ref[pl.ds(off, c), :], sf_ref[n], sb_ref[n]], axis=0)
            ys.append(jnp.dot(lhs, rhs, preferred_element_type=F32))
        for off, y in zip(offs, ys):
            y = y * lax.rsqrt(jnp.mean(y * y, axis=-1, keepdims=True) + NORM_EPS)
            o_ref[pl.ds(off, c), :] = (g_ref[pl.ds(off, c), :].astype(F32) * y).astype(o_ref.dtype)
        return carry

    lax.fori_loop(0, n_chunks // unroll, out_step, 0)


def _rotary_tables(l):
    half = HEAD_DIM // 2
    inv_freq = ROPE_BASE ** (-jnp.arange(half, dtype=F32) / half)
    ang = jnp.arange(l, dtype=F32)[:, None] * inv_freq[None, :]
    cos, sin = jnp.cos(ang), jnp.sin(ang)
    return jnp.concatenate([cos, cos], axis=-1), jnp.concatenate([-sin, sin], axis=-1)


def _retention(proj, decay_exp, tok_heads):
    b, l, _ = proj.shape
    assert l % RET_CHUNK == 0 and (l // RET_CHUNK) % min(RET_UNROLL, l // RET_CHUNK) == 0
    dec = jnp.broadcast_to(decay_exp.astype(F32).T[:, :, None], (tok_heads, 2, HEAD_DIM))
    head_spec = lambda part: pl.BlockSpec((None, l, HEAD_DIM), lambda bi, h: (bi, 0, part * tok_heads + h))
    n_chunks = l // RET_CHUNK
    return pl.pallas_call(
        functools.partial(_retention_kernel, n_chunks=n_chunks),
        grid=(b, tok_heads),
        in_specs=[pl.BlockSpec((None, 2, HEAD_DIM), lambda bi, h: (h, 0, 0)),
                  head_spec(0), head_spec(1), head_spec(2), head_spec(3)],
        out_specs=pl.BlockSpec((None, l, HEAD_DIM), lambda bi, h: (bi, 0, h)),
        out_shape=jax.ShapeDtypeStruct((b, l, tok_heads * HEAD_DIM), BF16),
        scratch_shapes=[pltpu.VMEM((n_chunks, 2 * HEAD_DIM, HEAD_DIM), F32),
                        pltpu.VMEM((n_chunks, HEAD_DIM, HEAD_DIM), BF16),
                        pltpu.VMEM((n_chunks, HEAD_DIM, HEAD_DIM), BF16)],
        compiler_params=_params("parallel", "parallel"),
        name="retention",
    )(dec, proj, proj, proj, proj)


def _na_kernel(bias_ref, q_ref, k_ref, v_ref, o_ref, *, rows):
    win = NA_KH * GRID_W
    unroll = min(NA_UNROLL, rows)

    def rows_step(i, carry):
        qrows = [i * unroll + u for u in range(unroll)]
        starts = [jnp.clip(r - NA_KH // 2, 0, rows - NA_KH) for r in qrows]
        q_offs = [pl.multiple_of(r * GRID_W, GRID_W) for r in qrows]
        k_offs = [pl.multiple_of(rs * GRID_W, GRID_W) for rs in starts]
        scores = [lax.dot_general(q_ref[pl.ds(q_off, GRID_W), :], k_ref[pl.ds(k_off, win), :], NT_DIMS,
                                  preferred_element_type=F32)
                  for q_off, k_off in zip(q_offs, k_offs)]
        probs, denoms = [], []
        for r, rs, s in zip(qrows, starts, scores):
            s = s + bias_ref[rs - r + NA_KH - 1]
            p = jnp.exp2(s - jnp.max(s, axis=-1, keepdims=True))
            denoms.append(jnp.sum(p, axis=-1, keepdims=True))
            probs.append(p.astype(BF16))
        outs = [jnp.dot(p, v_ref[pl.ds(k_off, win), :], preferred_element_type=F32)
                for p, k_off in zip(probs, k_offs)]
        for q_off, out, denom in zip(q_offs, outs, denoms):
            o_ref[pl.ds(q_off, GRID_W), :] = (out / denom).astype(o_ref.dtype)
        return carry

    lax.fori_loop(0, rows // unroll, rows_step, 0)


def _na_bias_table(rpb):
    c = np.arange(GRID_W)
    win_start = np.clip(c - NA_KW // 2, 0, GRID_W - NA_KW)
    valid = (c[None, :] >= win_start[:, None]) & (c[None, :] < win_start[:, None] + NA_KW)
    dc_idx = np.clip(c[None, :] - c[:, None] + NA_KW - 1, 0, 2 * NA_KW - 2)
    toeplitz = jnp.where(valid[None, None], rpb.astype(F32)[:, :, dc_idx] * LOG2_E, -jnp.inf)
    dr_idx = np.arange(NA_KH)[:, None] + np.arange(NA_KH)[None, :]
    table = toeplitz[:, dr_idx]
    table = table.transpose(0, 1, 3, 2, 4)
    return table.reshape(rpb.shape[0], NA_KH, GRID_W, NA_KH * GRID_W)


def _neighbourhood(proj, rpb, tok_heads):
    b, l, _ = proj.shape
    rows = l // GRID_W
    assert l % GRID_W == 0 and rows >= NA_KH and rows % min(NA_UNROLL, rows) == 0
    bias = _na_bias_table(rpb)
    head_spec = lambda part: pl.BlockSpec((None, l, HEAD_DIM), lambda bi, h: (bi, 0, part * tok_heads + h))
    return pl.pallas_call(
        functools.partial(_na_kernel, rows=rows),
        grid=(b, tok_heads),
        in_specs=[pl.BlockSpec((None, NA_KH, GRID_W, NA_KH * GRID_W), lambda bi, h: (h, 0, 0, 0)),
                  head_spec(0), head_spec(1), head_spec(2)],
        out_specs=pl.BlockSpec((None, l, HEAD_DIM), lambda bi, h: (bi, 0, h)),
        out_shape=jax.ShapeDtypeStruct((b, l, tok_heads * HEAD_DIM), BF16),
        compiler_params=_params("parallel", "parallel"),
        name="neighbourhood",
    )(bias, proj, proj, proj)


def _trunk(x, mem, norm_gain, mem_norm_gain, w_mem_kv, w_out, w_mlp_in, w_mlp_out, w_in_ret, ret_decay,
           w_in_na, na_rpb):
    b, l, d = x.shape
    n_mem = mem.shape[1]
    tok_width = w_out.shape[1] - MEM_WIDTH
    tok_heads = tok_width // HEAD_DIM
    xf = x.reshape(b * l, d)
    memf = mem.reshape(b * n_mem, d)
    for i in range(norm_gain.shape[0]):
        g = norm_gain[i]
        mem_kv = _norm_matmul(memf, mem_norm_gain[i], w_mem_kv, i).reshape(b, n_mem, 2 * MEM_WIDTH)
        j = i // N_MIXERS
        if i % N_MIXERS == 0:
            proj = _norm_matmul(xf, g[0], w_in_ret, j, columns="retention", tok_width=tok_width,
                                seq=l).reshape(b, l, -1)
            tok = _retention(proj, ret_decay[j], tok_heads)
        else:
            proj = _norm_matmul(xf, g[0], w_in_na, j, columns="neighbourhood", tok_width=tok_width,
                                seq=l).reshape(b, l, -1)
            tok = _neighbourhood(proj, na_rpb[j], tok_heads)
        x1 = _out_proj(tok.reshape(b * l, tok_width), proj.reshape(b * l, -1), mem_kv, xf, g[1], w_out, i)
        xf = _mlp(x1, g[2], g[3], w_mlp_in, w_mlp_out, i)
    return xf.reshape(b, l, d)


def kernel(x_prompt, x_sample, mem_prompt, mem_sample, norm_gain, mem_norm_gain, w_mem_kv, w_out, w_mlp_in,
           w_mlp_out, w_in_ret, ret_decay, w_in_na, na_rpb):
    weights = [w.astype(BF16) for w in (w_mem_kv, w_out, w_mlp_in, w_mlp_out, w_in_ret)]
    w_mem_kv, w_out, w_mlp_in, w_mlp_out, w_in_ret = weights
    w_in_na = w_in_na.astype(BF16)
    run = functools.partial(_trunk, norm_gain=norm_gain, mem_norm_gain=mem_norm_gain, w_mem_kv=w_mem_kv,
                            w_out=w_out, w_mlp_in=w_mlp_in, w_mlp_out=w_mlp_out, w_in_ret=w_in_ret,
                            ret_decay=ret_decay, w_in_na=w_in_na, na_rpb=na_rpb)
    return (run(x_prompt, mem_prompt), run(x_sample, mem_sample))
```

```python
import functools

import jax
import jax.numpy as jnp
import numpy as np
from jax import lax
from jax.experimental import pallas as pl
from jax.experimental.pallas import tpu as pltpu

HEAD_DIM = 128
MEM_HEADS = 4
MEM_WIDTH = MEM_HEADS * HEAD_DIM
N_MIXERS = 2
RET_CHUNK = 128
ROPE_BASE = 10000.0
GRID_W = 64
NA_KH = 8
NA_KW = 16
NORM_EPS = 1e-6
ATTN_SCALE = HEAD_DIM ** -0.5
LOG2_E = 1.4426950408889634
MLP_HIDDEN_TILE = 1024
MIXER_STEP_TOKENS = 8192
RET_UNROLL = 16
NA_UNROLL = 16

V7X_VMEM_BYTES = 64 * 1024 * 1024
VMEM_LIMIT_BYTES = V7X_VMEM_BYTES - 4 * 1024 * 1024

F32 = jnp.float32
BF16 = jnp.bfloat16
NT_DIMS = (((1,), (1,)), ((), ()))


def _params(*semantics):
    return pltpu.CompilerParams(dimension_semantics=semantics, vmem_limit_bytes=VMEM_LIMIT_BYTES)


def _rms(x, gain):
    return x * lax.rsqrt(jnp.mean(x * x, axis=-1, keepdims=True) + NORM_EPS) * gain


def _rotate(x, cos2, sin2):
    return x * cos2 + pltpu.roll(x, HEAD_DIM // 2, axis=1) * sin2


def _retention_columns(res, cos2, sin2, tok_width):
    heads = tok_width // HEAD_DIM
    blocks = []
    for c in range(res.shape[1] // HEAD_DIM):
        blk = res[:, c * HEAD_DIM:(c + 1) * HEAD_DIM]
        part = c // heads
        if part == 0:
            blk = _rotate(blk, cos2, sin2)
        elif part == 1:
            blk = _rotate(blk, cos2, sin2) * ATTN_SCALE
        elif part == 3:
            blk = blk * (1.0 / (1.0 + jnp.exp(-blk)))
        blocks.append(blk.astype(BF16))
    return jnp.concatenate(blocks, axis=1)


def _neighbourhood_columns(res, tok_width):
    q = res[:, :tok_width] * (ATTN_SCALE * LOG2_E)
    return jnp.concatenate([q.astype(BF16), res[:, tok_width:].astype(BF16)], axis=1)


def _norm_matmul_kernel(x_ref, g_ref, w_ref, *rest, sub_rows, columns, tok_width):
    o_ref = rest[-1]
    gain = g_ref[...]
    for r in range(x_ref.shape[0] // sub_rows):
        rows = slice(r * sub_rows, (r + 1) * sub_rows)
        h = _rms(x_ref[rows, :], gain).astype(BF16)
        res = jnp.dot(h, w_ref[...], preferred_element_type=F32)
        if columns == "retention":
            cos_ref, sin_ref = rest[0], rest[1]
            res = _retention_columns(res, cos_ref[rows, :], sin_ref[rows, :], tok_width)
        elif columns == "neighbourhood":
            res = _neighbourhood_columns(res, tok_width)
        o_ref[rows, :] = res.astype(o_ref.dtype)


def _norm_matmul(x, gain, w, layer, *, columns=None, tok_width=None, seq=None, tm=512, sub_rows=256):
    t, d = x.shape
    n = w.shape[2]
    tm = min(tm, t if seq is None else seq)
    sub_rows = min(sub_rows, tm)
    assert t % tm == 0 and tm % sub_rows == 0
    operands = [x, gain.reshape(1, d), w]
    in_specs = [
        pl.BlockSpec((tm, d), lambda i: (i, 0)),
        pl.BlockSpec((1, d), lambda i: (0, 0)),
        pl.BlockSpec((None, d, n), lambda i: (layer, 0, 0), pipeline_mode=pl.Buffered(1)),
    ]
    if columns == "retention":
        assert seq % tm == 0
        tiles_per_seq = seq // tm
        operands += list(_rotary_tables(seq))
        in_specs += [pl.BlockSpec((tm, HEAD_DIM), lambda i: (i % tiles_per_seq, 0)),
                     pl.BlockSpec((tm, HEAD_DIM), lambda i: (i % tiles_per_seq, 0))]
    return pl.pallas_call(
        functools.partial(_norm_matmul_kernel, sub_rows=sub_rows, columns=columns, tok_width=tok_width),
        grid=(t // tm,),
        in_specs=in_specs,
        out_specs=pl.BlockSpec((tm, n), lambda i: (i, 0)),
        out_shape=jax.ShapeDtypeStruct((t, n), BF16),
        compiler_params=_params("parallel"),
        name="norm_matmul",
    )(*operands)


def _mem_scores(q, k_ref):
    return [lax.dot_general(q[:, h * HEAD_DIM:(h + 1) * HEAD_DIM], k_ref[:, h * HEAD_DIM:(h + 1) * HEAD_DIM],
                            NT_DIMS, preferred_element_type=F32) for h in range(MEM_HEADS)]


def _mem_values(scores, v_ref):
    probs, denoms = [], []
    for s in scores:
        s = s * ATTN_SCALE
        p = jnp.exp(s - jnp.max(s, axis=-1, keepdims=True))
        denoms.append(jnp.sum(p, axis=-1, keepdims=True))
        probs.append(p.astype(BF16))
    outs = [jnp.dot(p, v_ref[:, h * HEAD_DIM:(h + 1) * HEAD_DIM], preferred_element_type=F32)
            for h, p in enumerate(probs)]
    return jnp.concatenate([(o / dn).astype(BF16) for o, dn in zip(outs, denoms)], axis=1)


def _out_proj_kernel(tok_ref, qm_ref, k_ref, v_ref, x_ref, g_ref, wt_ref, wm_ref, o_ref, *, sub_rows):
    gain = g_ref[...]
    n_sub = x_ref.shape[0] // sub_rows
    rows = [slice(r * sub_rows, (r + 1) * sub_rows) for r in range(n_sub)]
    mem = _mem_values(_mem_scores(qm_ref[rows[0], :], k_ref), v_ref)
    for r in range(n_sub):
        if r + 1 < n_sub:
            next_scores = _mem_scores(qm_ref[rows[r + 1], :], k_ref)
        mixed = (jnp.dot(tok_ref[rows[r], :], wt_ref[...], preferred_element_type=F32)
                 + jnp.dot(mem, wm_ref[...], preferred_element_type=F32))
        if r + 1 < n_sub:
            mem = _mem_values(next_scores, v_ref)
        o_ref[rows[r], :] = x_ref[rows[r], :] + _rms(mixed, gain)


def _out_proj(tok, proj, mem_kv, x, gain, w_out, layer, *, tm=1024, sub_rows=256):
    t, d = x.shape
    b, m, _ = mem_kv.shape
    seq = t // b
    tm = min(tm, seq)
    sub_rows = min(sub_rows, tm)
    tw, n = tok.shape[1], proj.shape[1]
    assert seq % tm == 0 and tm % sub_rows == 0 and tw % MEM_WIDTH == 0 and n % MEM_WIDTH == 0
    assert w_out.shape[1] == tw + MEM_WIDTH
    tiles_per_seq = seq // tm
    q_block = n // MEM_WIDTH - 1
    return pl.pallas_call(
        functools.partial(_out_proj_kernel, sub_rows=sub_rows),
        grid=(t // tm,),
        in_specs=[
            pl.BlockSpec((tm, tw), lambda i: (i, 0)),
            pl.BlockSpec((tm, MEM_WIDTH), lambda i: (i, q_block)),
            pl.BlockSpec((None, m, MEM_WIDTH), lambda i: (i // tiles_per_seq, 0, 0)),
            pl.BlockSpec((None, m, MEM_WIDTH), lambda i: (i // tiles_per_seq, 0, 1)),
            pl.BlockSpec((tm, d), lambda i: (i, 0)),
            pl.BlockSpec((1, d), lambda i: (0, 0)),
            pl.BlockSpec((None, tw, d), lambda i: (layer, 0, 0), pipeline_mode=pl.Buffered(1)),
            pl.BlockSpec((None, MEM_WIDTH, d), lambda i: (layer, tw // MEM_WIDTH, 0),
                         pipeline_mode=pl.Buffered(1)),
        ],
        out_specs=pl.BlockSpec((tm, d), lambda i: (i, 0)),
        out_shape=jax.ShapeDtypeStruct((t, d), F32),
        compiler_params=_params("parallel"),
        name="out_proj",
    )(tok, proj, mem_kv, mem_kv, x, gain.reshape(1, d), w_out, w_out)


def _mlp_kernel(x_ref, gi_ref, go_ref, w1_ref, w2_ref, o_ref, h_ref, *, sub_rows):
    j = pl.program_id(1)
    last = pl.num_programs(1) - 1
    n_sub = x_ref.shape[0] // sub_rows

    def step(first, final):
        hidden = []
        for r in range(n_sub):
            rows = slice(r * sub_rows, (r + 1) * sub_rows)
            if first:
                h = _rms(x_ref[rows, :], gi_ref[...]).astype(BF16)
                h_ref[rows, :] = h
            else:
                h = h_ref[rows, :]
            u = jnp.dot(h, w1_ref[...], preferred_element_type=F32)
            hidden.append(jnp.square(jnp.maximum(u, 0.0)).astype(BF16))
        for r in range(n_sub):
            rows = slice(r * sub_rows, (r + 1) * sub_rows)
            acc = jnp.dot(hidden[r], w2_ref[...], preferred_element_type=F32)
            if not first:
                acc = o_ref[rows, :] + acc
            if final:
                acc = x_ref[rows, :] + _rms(acc, go_ref[...])
            o_ref[rows, :] = acc

    pl.when(j == 0)(lambda: step(True, False))
    pl.when(jnp.logical_and(j > 0, j < last))(lambda: step(False, False))
    pl.when(j == last)(lambda: step(False, True))


def _mlp(x, gain_in, gain_out, w1, w2, layer, *, tm=1024, tf=MLP_HIDDEN_TILE, sub_rows=512):
    t, d = x.shape
    f = w1.shape[2]
    tm = min(tm, t)
    sub_rows = min(sub_rows, tm)
    assert t % tm == 0 and tm % sub_rows == 0 and f % tf == 0 and f // tf >= 2
    return pl.pallas_call(
        functools.partial(_mlp_kernel, sub_rows=sub_rows),
        grid=(t // tm, f // tf),
        in_specs=[
            pl.BlockSpec((tm, d), lambda i, j: (i, 0)),
            pl.BlockSpec((1, d), lambda i, j: (0, 0)),
            pl.BlockSpec((1, d), lambda i, j: (0, 0)),
            pl.BlockSpec((None, d, tf), lambda i, j: (layer, 0, j)),
            pl.BlockSpec((None, tf, d), lambda i, j: (layer, j, 0)),
        ],
        out_specs=pl.BlockSpec((tm, d), lambda i, j: (i, 0)),
        out_shape=jax.ShapeDtypeStruct((t, d), F32),
        scratch_shapes=[pltpu.VMEM((tm, d), BF16)],
        compiler_params=_params("parallel", "arbitrary"),
        name="mlp",
    )(x, gain_in.reshape(1, d), gain_out.reshape(1, d), w1, w2)


def _sequences_per_step(b, l):
    nb = max(1, min(b, MIXER_STEP_TOKENS // l))
    while b % nb:
        nb -= 1
    return nb


def _retention_kernel(dec_ref, q_ref, k_ref, v_ref, g_ref, o_ref, kv_ref, sf_ref, sb_ref, *, n_chunks):
    c = RET_CHUNK
    unroll = min(RET_UNROLL, n_chunks)
    log_gamma = jnp.log1p(-jnp.exp2(-dec_ref[...]))
    lgf, lgb = log_gamma[0:1, :], log_gamma[1:2, :]
    lane = lax.broadcasted_iota(jnp.int32, (1, c), 1).astype(F32)
    k_decay_f_row = jnp.exp((c - 1.0 - lane) * lgf)
    k_decay_b_row = jnp.exp(lane * lgb)
    chunk_decay_f = jnp.exp(c * lgf)
    chunk_decay_b = jnp.exp(c * lgb)
    row = lax.broadcasted_iota(jnp.int32, (c, c), 0).astype(F32)
    col = lax.broadcasted_iota(jnp.int32, (c, c), 1).astype(F32)
    diff = row - col
    decay_intra = jnp.exp(jnp.where(diff >= 0, diff * lgf, -diff * lgb))
    pos = lax.broadcasted_iota(jnp.int32, (c, HEAD_DIM), 0).astype(F32)
    q_decay_f = jnp.exp((pos + 1.0) * lgf)
    q_decay_b = jnp.exp((c - pos) * lgb)
    zero_state = jnp.zeros((HEAD_DIM, HEAD_DIM), F32)

    def one_sequence(q, k, v, g, o):
        def kv_step(i, carry):
            offs = [pl.multiple_of((i * unroll + u) * c, c) for u in range(unroll)]
            lhs = []
            for off in offs:
                kr_t = k[pl.ds(off, c), :].astype(F32).T
                lhs.append(jnp.concatenate([kr_t * k_decay_f_row, kr_t * k_decay_b_row], axis=0).astype(BF16))
            for u, off in enumerate(offs):
                kv_ref[i * unroll + u] = jnp.dot(lhs[u], v[pl.ds(off, c), :], preferred_element_type=F32)
            return carry

        lax.fori_loop(0, n_chunks // unroll, kv_step, 0)

        def scan_step(t, carry):
            sf, sb = carry
            nb = n_chunks - 1 - t
            sf_ref[t] = sf.astype(BF16)
            sb_ref[nb] = sb.astype(BF16)
            sf = chunk_decay_f * sf + kv_ref[t, 0:HEAD_DIM, :]
            sb = chunk_decay_b * sb + kv_ref[nb, HEAD_DIM:2 * HEAD_DIM, :]
            return sf, sb

        lax.fori_loop(0, n_chunks, scan_step, (zero_state, zero_state), unroll=unroll)

        def out_step(i, carry):
            chunks = [i * unroll + u for u in range(unroll)]
            offs = [pl.multiple_of(n * c, c) for n in chunks]
            scores, ys = [], []
            for off in offs:
                scores.append(lax.dot_general(q[pl.ds(off, c), :], k[pl.ds(off, c), :], NT_DIMS,
                                              preferred_element_type=F32))
            for n, off, s in zip(chunks, offs, scores):
                qr = q[pl.ds(off, c), :].astype(F32)
                lhs = jnp.concatenate([s * decay_intra, qr * q_decay_f, qr * q_decay_b], axis=1).astype(BF16)
                rhs = jnp.concatenate([v[pl.ds(off, c), :], sf_ref[n], sb_ref[n]], axis=0)
                ys.append(jnp.dot(lhs, rhs, preferred_element_type=F32))
            for off, y in zip(offs, ys):
                y = y * lax.rsqrt(jnp.mean(y * y, axis=-1, keepdims=True) + NORM_EPS)
                o[pl.ds(off, c), :] = (g[pl.ds(off, c), :].astype(F32) * y).astype(o.dtype)
            return carry

        lax.fori_loop(0, n_chunks // unroll, out_step, 0)

    def sequence_step(s, carry):
        one_sequence(q_ref.at[s], k_ref.at[s], v_ref.at[s], g_ref.at[s], o_ref.at[s])
        return carry

    lax.fori_loop(0, q_ref.shape[0], sequence_step, 0)


def _rotary_tables(l):
    half = HEAD_DIM // 2
    inv_freq = ROPE_BASE ** (-jnp.arange(half, dtype=F32) / half)
    ang = jnp.arange(l, dtype=F32)[:, None] * inv_freq[None, :]
    cos, sin = jnp.cos(ang), jnp.sin(ang)
    return jnp.concatenate([cos, cos], axis=-1), jnp.concatenate([-sin, sin], axis=-1)


def _retention(proj, decay_exp, tok_heads):
    b, l, _ = proj.shape
    assert l % RET_CHUNK == 0 and (l // RET_CHUNK) % min(RET_UNROLL, l // RET_CHUNK) == 0
    dec = jnp.broadcast_to(decay_exp.astype(F32).T[:, :, None], (tok_heads, 2, HEAD_DIM))
    nb = _sequences_per_step(b, l)
    head_spec = lambda part: pl.BlockSpec((nb, l, HEAD_DIM), lambda bi, h: (bi, 0, part * tok_heads + h))
    n_chunks = l // RET_CHUNK
    return pl.pallas_call(
        functools.partial(_retention_kernel, n_chunks=n_chunks),
        grid=(b // nb, tok_heads),
        in_specs=[pl.BlockSpec((None, 2, HEAD_DIM), lambda bi, h: (h, 0, 0)),
                  head_spec(0), head_spec(1), head_spec(2), head_spec(3)],
        out_specs=pl.BlockSpec((nb, l, HEAD_DIM), lambda bi, h: (bi, 0, h)),
        out_shape=jax.ShapeDtypeStruct((b, l, tok_heads * HEAD_DIM), BF16),
        scratch_shapes=[pltpu.VMEM((n_chunks, 2 * HEAD_DIM, HEAD_DIM), F32),
                        pltpu.VMEM((n_chunks, HEAD_DIM, HEAD_DIM), BF16),
                        pltpu.VMEM((n_chunks, HEAD_DIM, HEAD_DIM), BF16)],
        compiler_params=_params("parallel", "parallel"),
        name="retention",
    )(dec, proj, proj, proj, proj)


def _na_kernel(bias_ref, q_ref, k_ref, v_ref, o_ref, *, rows):
    win = NA_KH * GRID_W
    unroll = min(NA_UNROLL, rows)

    def one_sequence(q, k, v, o):
        def rows_step(i, carry):
            qrows = [i * unroll + u for u in range(unroll)]
            starts = [jnp.clip(r - NA_KH // 2, 0, rows - NA_KH) for r in qrows]
            q_offs = [pl.multiple_of(r * GRID_W, GRID_W) for r in qrows]
            k_offs = [pl.multiple_of(rs * GRID_W, GRID_W) for rs in starts]
            scores = [lax.dot_general(q[pl.ds(q_off, GRID_W), :], k[pl.ds(k_off, win), :], NT_DIMS,
                                      preferred_element_type=F32)
                      for q_off, k_off in zip(q_offs, k_offs)]
            probs, denoms = [], []
            for r, rs, s in zip(qrows, starts, scores):
                s = s + bias_ref[rs - r + NA_KH - 1]
                p = jnp.exp2(s - jnp.max(s, axis=-1, keepdims=True))
                denoms.append(jnp.sum(p, axis=-1, keepdims=True))
                probs.append(p.astype(BF16))
            outs = [jnp.dot(p, v[pl.ds(k_off, win), :], preferred_element_type=F32)
                    for p, k_off in zip(probs, k_offs)]
            for q_off, out, denom in zip(q_offs, outs, denoms):
                o[pl.ds(q_off, GRID_W), :] = (out / denom).astype(o.dtype)
            return carry

        lax.fori_loop(0, rows // unroll, rows_step, 0)

    def sequence_step(s, carry):
        one_sequence(q_ref.at[s], k_ref.at[s], v_ref.at[s], o_ref.at[s])
        return carry

    lax.fori_loop(0, q_ref.shape[0], sequence_step, 0)


def _na_bias_table(rpb):
    n_dc = 2 * NA_KW - 1
    o, c, t, c2 = np.meshgrid(np.arange(NA_KH), np.arange(GRID_W), np.arange(NA_KH), np.arange(GRID_W),
                              indexing="ij")
    win_start = np.clip(c - NA_KW // 2, 0, GRID_W - NA_KW)
    valid = (c2 >= win_start) & (c2 < win_start + NA_KW)
    flat_idx = (o + t) * n_dc + np.clip(c2 - c + NA_KW - 1, 0, n_dc - 1)
    shape = (NA_KH, GRID_W, NA_KH * GRID_W)
    flat_idx, valid = flat_idx.reshape(shape), valid.reshape(shape)
    heads = rpb.shape[0]
    entries = jnp.take(rpb.astype(F32).reshape(heads, -1), jnp.asarray(flat_idx.astype(np.int32)), axis=1)
    return jnp.where(jnp.asarray(valid)[None], entries * LOG2_E, -jnp.inf)


def _neighbourhood(proj, rpb, tok_heads):
    b, l, _ = proj.shape
    rows = l // GRID_W
    assert l % GRID_W == 0 and rows >= NA_KH and rows % min(NA_UNROLL, rows) == 0
    bias = _na_bias_table(rpb)
    nb = _sequences_per_step(b, l)
    head_spec = lambda part: pl.BlockSpec((nb, l, HEAD_DIM), lambda bi, h: (bi, 0, part * tok_heads + h))
    return pl.pallas_call(
        functools.partial(_na_kernel, rows=rows),
        grid=(b // nb, tok_heads),
        in_specs=[pl.BlockSpec((None, NA_KH, GRID_W, NA_KH * GRID_W), lambda bi, h: (h, 0, 0, 0)),
                  head_spec(0), head_spec(1), head_spec(2)],
        out_specs=pl.BlockSpec((nb, l, HEAD_DIM), lambda bi, h: (bi, 0, h)),
        out_shape=jax.ShapeDtypeStruct((b, l, tok_heads * HEAD_DIM), BF16),
        compiler_params=_params("parallel", "parallel"),
        name="neighbourhood",
    )(bias, proj, proj, proj)


def _trunk(x, mem, norm_gain, mem_norm_gain, w_mem_kv, w_out, w_mlp_in, w_mlp_out, w_in_ret, ret_decay,
           w_in_na, na_rpb):
    b, l, d = x.shape
    n_mem = mem.shape[1]
    tok_width = w_out.shape[1] - MEM_WIDTH
    tok_heads = tok_width // HEAD_DIM
    xf = x.reshape(b * l, d)
    memf = mem.reshape(b * n_mem, d)
    for i in range(norm_gain.shape[0]):
        g = norm_gain[i]
        mem_kv = _norm_matmul(memf, mem_norm_gain[i], w_mem_kv, i).reshape(b, n_mem, 2 * MEM_WIDTH)
        j = i // N_MIXERS
        if i % N_MIXERS == 0:
            proj = _norm_matmul(xf, g[0], w_in_ret, j, columns="retention", tok_width=tok_width,
                                seq=l).reshape(b, l, -1)
            tok = _retention(proj, ret_decay[j], tok_heads)
        else:
            proj = _norm_matmul(xf, g[0], w_in_na, j, columns="neighbourhood", tok_width=tok_width,
                                seq=l).reshape(b, l, -1)
            tok = _neighbourhood(proj, na_rpb[j], tok_heads)
        x1 = _out_proj(tok.reshape(b * l, tok_width), proj.reshape(b * l, -1), mem_kv, xf, g[1], w_out, i)
        xf = _mlp(x1, g[2], g[3], w_mlp_in, w_mlp_out, i)
    return xf.reshape(b, l, d)


def kernel(x_prompt, x_sample, mem_prompt, mem_sample, norm_gain, mem_norm_gain, w_mem_kv, w_out, w_mlp_in,
           w_mlp_out, w_in_ret, ret_decay, w_in_na, na_rpb):
    weights = [w.astype(BF16) for w in (w_mem_kv, w_out, w_mlp_in, w_mlp_out, w_in_ret)]
    w_mem_kv, w_out, w_mlp_in, w_mlp_out, w_in_ret = weights
    w_in_na = w_in_na.astype(BF16)
    run = functools.partial(_trunk, norm_gain=norm_gain, mem_norm_gain=mem_norm_gain, w_mem_kv=w_mem_kv,
                            w_out=w_out, w_mlp_in=w_mlp_in, w_mlp_out=w_mlp_out, w_in_ret=w_in_ret,
                            ret_decay=ret_decay, w_in_na=w_in_na, na_rpb=na_rpb)
    return (run(x_prompt, mem_prompt), run(x_sample, mem_sample))
```

```python
import functools

import jax
import jax.numpy as jnp
import numpy as np
from jax import lax
from jax.experimental import pallas as pl
from jax.experimental.pallas import tpu as pltpu

HEAD_DIM = 128
MEM_HEADS = 4
MEM_WIDTH = MEM_HEADS * HEAD_DIM
N_MIXERS = 2
RET_CHUNK = 128
ROPE_BASE = 10000.0
GRID_W = 64
NA_KH = 8
NA_KW = 16
NORM_EPS = 1e-6
ATTN_SCALE = HEAD_DIM ** -0.5
LOG2_E = 1.4426950408889634
MLP_HIDDEN_TILE = 1024
MIXER_STEP_TOKENS = 8192
RET_UNROLL = 16
NA_UNROLL = 16

V7X_VMEM_BYTES = 64 * 1024 * 1024
VMEM_LIMIT_BYTES = V7X_VMEM_BYTES - 4 * 1024 * 1024

F32 = jnp.float32
BF16 = jnp.bfloat16
NT_DIMS = (((1,), (1,)), ((), ()))


def _params(*semantics):
    return pltpu.CompilerParams(dimension_semantics=semantics, vmem_limit_bytes=VMEM_LIMIT_BYTES)


def _rms(x, gain):
    return x * lax.rsqrt(jnp.mean(x * x, axis=-1, keepdims=True) + NORM_EPS) * gain


def _rotate(x, cos2, sin2):
    return x * cos2 + pltpu.roll(x, HEAD_DIM // 2, axis=1) * sin2


def _retention_columns(res, cos2, sin2, tok_width):
    heads = tok_width // HEAD_DIM
    blocks = []
    for c in range(res.shape[1] // HEAD_DIM):
        blk = res[:, c * HEAD_DIM:(c + 1) * HEAD_DIM]
        part = c // heads
        if part == 0:
            blk = _rotate(blk, cos2, sin2)
        elif part == 1:
            blk = _rotate(blk, cos2, sin2) * ATTN_SCALE
        elif part == 3:
            blk = blk * (1.0 / (1.0 + jnp.exp(-blk)))
        blocks.append(blk.astype(BF16))
    return jnp.concatenate(blocks, axis=1)


def _neighbourhood_columns(res, tok_width):
    q = res[:, :tok_width] * (ATTN_SCALE * LOG2_E)
    return jnp.concatenate([q.astype(BF16), res[:, tok_width:].astype(BF16)], axis=1)


def _norm_matmul_kernel(x_ref, g_ref, w_ref, *rest, sub_rows, columns, tok_width):
    o_ref = rest[-1]
    gain = g_ref[...]
    for r in range(x_ref.shape[0] // sub_rows):
        rows = slice(r * sub_rows, (r + 1) * sub_rows)
        h = _rms(x_ref[rows, :], gain).astype(BF16)
        res = jnp.dot(h, w_ref[...], preferred_element_type=F32)
        if columns == "retention":
            cos_ref, sin_ref = rest[0], rest[1]
            res = _retention_columns(res, cos_ref[rows, :], sin_ref[rows, :], tok_width)
        elif columns == "neighbourhood":
            res = _neighbourhood_columns(res, tok_width)
        o_ref[rows, :] = res.astype(o_ref.dtype)


def _norm_matmul(x, gain, w, layer, *, columns=None, tok_width=None, seq=None, tm=512, sub_rows=256):
    t, d = x.shape
    n = w.shape[2]
    tm = min(tm, t if seq is None else seq)
    sub_rows = min(sub_rows, tm)
    assert t % tm == 0 and tm % sub_rows == 0
    operands = [x, gain.reshape(1, d), w]
    in_specs = [
        pl.BlockSpec((tm, d), lambda i: (i, 0)),
        pl.BlockSpec((1, d), lambda i: (0, 0)),
        pl.BlockSpec((None, d, n), lambda i: (layer, 0, 0), pipeline_mode=pl.Buffered(1)),
    ]
    if columns == "retention":
        assert seq % tm == 0
        tiles_per_seq = seq // tm
        operands += list(_rotary_tables(seq))
        in_specs += [pl.BlockSpec((tm, HEAD_DIM), lambda i: (i % tiles_per_seq, 0)),
                     pl.BlockSpec((tm, HEAD_DIM), lambda i: (i % tiles_per_seq, 0))]
    return pl.pallas_call(
        functools.partial(_norm_matmul_kernel, sub_rows=sub_rows, columns=columns, tok_width=tok_width),
        grid=(t // tm,),
        in_specs=in_specs,
        out_specs=pl.BlockSpec((tm, n), lambda i: (i, 0)),
        out_shape=jax.ShapeDtypeStruct((t, n), BF16),
        compiler_params=_params("parallel"),
        name="norm_matmul",
    )(*operands)


def _mem_scores(q, k_ref):
    return [lax.dot_general(q[:, h * HEAD_DIM:(h + 1) * HEAD_DIM], k_ref[:, h * HEAD_DIM:(h + 1) * HEAD_DIM],
                            NT_DIMS, preferred_element_type=F32) for h in range(MEM_HEADS)]


def _mem_values(scores, v_ref):
    probs, denoms = [], []
    for s in scores:
        s = s * ATTN_SCALE
        p = jnp.exp(s - jnp.max(s, axis=-1, keepdims=True))
        denoms.append(jnp.sum(p, axis=-1, keepdims=True))
        probs.append(p.astype(BF16))
    outs = [jnp.dot(p, v_ref[:, h * HEAD_DIM:(h + 1) * HEAD_DIM], preferred_element_type=F32)
            for h, p in enumerate(probs)]
    return jnp.concatenate([(o / dn).astype(BF16) for o, dn in zip(outs, denoms)], axis=1)


def _out_proj_kernel(tok_ref, qm_ref, k_ref, v_ref, x_ref, g_ref, wt_ref, wm_ref, o_ref, *, sub_rows):
    gain = g_ref[...]
    n_sub = x_ref.shape[0] // sub_rows
    rows = [slice(r * sub_rows, (r + 1) * sub_rows) for r in range(n_sub)]
    mem = _mem_values(_mem_scores(qm_ref[rows[0], :], k_ref), v_ref)
    for r in range(n_sub):
        if r + 1 < n_sub:
            next_scores = _mem_scores(qm_ref[rows[r + 1], :], k_ref)
        mixed = (jnp.dot(tok_ref[rows[r], :], wt_ref[...], preferred_element_type=F32)
                 + jnp.dot(mem, wm_ref[...], preferred_element_type=F32))
        if r + 1 < n_sub:
            mem = _mem_values(next_scores, v_ref)
        o_ref[rows[r], :] = x_ref[rows[r], :] + _rms(mixed, gain)


def _out_proj(tok, proj, mem_kv, x, gain, w_out, layer, *, tm=1024, sub_rows=256):
    t, d = x.shape
    b, m, _ = mem_kv.shape
    seq = t // b
    tm = min(tm, seq)
    sub_rows = min(sub_rows, tm)
    tw, n = tok.shape[1], proj.shape[1]
    assert seq % tm == 0 and tm % sub_rows == 0 and tw % MEM_WIDTH == 0 and n % MEM_WIDTH == 0
    assert w_out.shape[1] == tw + MEM_WIDTH
    tiles_per_seq = seq // tm
    q_block = n // MEM_WIDTH - 1
    return pl.pallas_call(
        functools.partial(_out_proj_kernel, sub_rows=sub_rows),
        grid=(t // tm,),
        in_specs=[
            pl.BlockSpec((tm, tw), lambda i: (i, 0)),
            pl.BlockSpec((tm, MEM_WIDTH), lambda i: (i, q_block)),
            pl.BlockSpec((None, m, MEM_WIDTH), lambda i: (i // tiles_per_seq, 0, 0)),
            pl.BlockSpec((None, m, MEM_WIDTH), lambda i: (i // tiles_per_seq, 0, 1)),
            pl.BlockSpec((tm, d), lambda i: (i, 0)),
            pl.BlockSpec((1, d), lambda i: (0, 0)),
            pl.BlockSpec((None, tw, d), lambda i: (layer, 0, 0), pipeline_mode=pl.Buffered(1)),
            pl.BlockSpec((None, MEM_WIDTH, d), lambda i: (layer, tw // MEM_WIDTH, 0),
                         pipeline_mode=pl.Buffered(1)),
        ],
        out_specs=pl.BlockSpec((tm, d), lambda i: (i, 0)),
        out_shape=jax.ShapeDtypeStruct((t, d), F32),
        compiler_params=_params("parallel"),
        name="out_proj",
    )(tok, proj, mem_kv, mem_kv, x, gain.reshape(1, d), w_out, w_out)


def _mlp_kernel(x_ref, gi_ref, go_ref, w1_ref, w2_ref, o_ref, h_ref, *, sub_rows):
    j = pl.program_id(1)
    last = pl.num_programs(1) - 1
    n_sub = x_ref.shape[0] // sub_rows

    def step(first, final):
        hidden = []
        for r in range(n_sub):
            rows = slice(r * sub_rows, (r + 1) * sub_rows)
            if first:
                h = _rms(x_ref[rows, :], gi_ref[...]).astype(BF16)
                h_ref[rows, :] = h
            else:
                h = h_ref[rows, :]
            u = jnp.dot(h, w1_ref[...], preferred_element_type=F32)
            hidden.append(jnp.square(jnp.maximum(u, 0.0)).astype(BF16))
        for r in range(n_sub):
            rows = slice(r * sub_rows, (r + 1) * sub_rows)
            acc = jnp.dot(hidden[r], w2_ref[...], preferred_element_type=F32)
            if not first:
                acc = o_ref[rows, :] + acc
            if final:
                acc = x_ref[rows, :] + _rms(acc, go_ref[...])
            o_ref[rows, :] = acc

    pl.when(j == 0)(lambda: step(True, False))
    pl.when(jnp.logical_and(j > 0, j < last))(lambda: step(False, False))
    pl.when(j == last)(lambda: step(False, True))


def _mlp(x, gain_in, gain_out, w1, w2, layer, *, tm=1024, tf=MLP_HIDDEN_TILE, sub_rows=512):
    t, d = x.shape
    f = w1.shape[2]
    tm = min(tm, t)
    sub_rows = min(sub_rows, tm)
    assert t % tm == 0 and tm % sub_rows == 0 and f % tf == 0 and f // tf >= 2
    return pl.pallas_call(
        functools.partial(_mlp_kernel, sub_rows=sub_rows),
        grid=(t // tm, f // tf),
        in_specs=[
            pl.BlockSpec((tm, d), lambda i, j: (i, 0)),
            pl.BlockSpec((1, d), lambda i, j: (0, 0)),
            pl.BlockSpec((1, d), lambda i, j: (0, 0)),
            pl.BlockSpec((None, d, tf), lambda i, j: (layer, 0, j)),
            pl.BlockSpec((None, tf, d), lambda i, j: (layer, j, 0)),
        ],
        out_specs=pl.BlockSpec((tm, d), lambda i, j: (i, 0)),
        out_shape=jax.ShapeDtypeStruct((t, d), F32),
        scratch_shapes=[pltpu.VMEM((tm, d), BF16)],
        compiler_params=_params("parallel", "arbitrary"),
        name="mlp",
    )(x, gain_in.reshape(1, d), gain_out.reshape(1, d), w1, w2)


def _sequences_per_step(b, l):
    nb = max(1, min(b, MIXER_STEP_TOKENS // l))
    while b % nb:
        nb -= 1
    return nb


def _retention_kernel(dec_ref, q_ref, k_ref, v_ref, g_ref, o_ref, kvb_ref, sf_ref, *, n_chunks):
    c = RET_CHUNK
    unroll = min(RET_UNROLL, n_chunks)
    log_gamma = jnp.log1p(-jnp.exp2(-dec_ref[...]))
    lgf, lgb = log_gamma[0:1, :], log_gamma[1:2, :]
    lane = lax.broadcasted_iota(jnp.int32, (1, c), 1).astype(F32)
    k_decay_f_row = jnp.exp((c - 1.0 - lane) * lgf)
    k_decay_b_row = jnp.exp(lane * lgb)
    chunk_decay_f = jnp.exp(c * lgf)
    chunk_decay_b = jnp.exp(c * lgb)
    row = lax.broadcasted_iota(jnp.int32, (c, c), 0).astype(F32)
    col = lax.broadcasted_iota(jnp.int32, (c, c), 1).astype(F32)
    diff = row - col
    decay_intra = jnp.exp(jnp.where(diff >= 0, diff * lgf, -diff * lgb))
    pos = lax.broadcasted_iota(jnp.int32, (c, HEAD_DIM), 0).astype(F32)
    q_decay_f = jnp.exp((pos + 1.0) * lgf)
    q_decay_b = jnp.exp((c - pos) * lgb)
    zero_state = jnp.zeros((HEAD_DIM, HEAD_DIM), F32)

    def one_sequence(q, k, v, g, o):
        def forward_group(i, sf):
            chunks = [i * unroll + u for u in range(unroll)]
            offs = [pl.multiple_of(n * c, c) for n in chunks]
            lhs = []
            for off in offs:
                kr_t = k[pl.ds(off, c), :].astype(F32).T
                lhs.append(jnp.concatenate([kr_t * k_decay_f_row, kr_t * k_decay_b_row], axis=0).astype(BF16))
            kvs = [jnp.dot(lhs[u], v[pl.ds(off, c), :], preferred_element_type=F32)
                   for u, off in enumerate(offs)]
            for n, kv in zip(chunks, kvs):
                sf_ref[n] = sf.astype(BF16)
                sf = chunk_decay_f * sf + kv[0:HEAD_DIM, :]
                kvb_ref[n] = kv[HEAD_DIM:2 * HEAD_DIM, :]
            return sf

        lax.fori_loop(0, n_chunks // unroll, forward_group, zero_state)

        def backward_group(t, sb):
            i = n_chunks // unroll - 1 - t
            chunks = [i * unroll + u for u in range(unroll)]
            offs = [pl.multiple_of(n * c, c) for n in chunks]
            scores = [lax.dot_general(q[pl.ds(off, c), :], k[pl.ds(off, c), :], NT_DIMS,
                                      preferred_element_type=F32) for off in offs]
            states = [None] * unroll
            for u in reversed(range(unroll)):
                states[u] = sb.astype(BF16)
                sb = chunk_decay_b * sb + kvb_ref[chunks[u]]
            ys = []
            for n, off, s, sb_n in zip(chunks, offs, scores, states):
                qr = q[pl.ds(off, c), :].astype(F32)
                lhs = jnp.concatenate([s * decay_intra, qr * q_decay_f, qr * q_decay_b], axis=1).astype(BF16)
                rhs = jnp.concatenate([v[pl.ds(off, c), :], sf_ref[n], sb_n], axis=0)
                ys.append(jnp.dot(lhs, rhs, preferred_element_type=F32))
            for off, y in zip(offs, ys):
                y = y * lax.rsqrt(jnp.mean(y * y, axis=-1, keepdims=True) + NORM_EPS)
                o[pl.ds(off, c), :] = (g[pl.ds(off, c), :].astype(F32) * y).astype(o.dtype)
            return sb

        lax.fori_loop(0, n_chunks // unroll, backward_group, zero_state)

    def sequence_step(s, carry):
        one_sequence(q_ref.at[s], k_ref.at[s], v_ref.at[s], g_ref.at[s], o_ref.at[s])
        return carry

    lax.fori_loop(0, q_ref.shape[0], sequence_step, 0)


def _rotary_tables(l):
    half = HEAD_DIM // 2
    inv_freq = ROPE_BASE ** (-jnp.arange(half, dtype=F32) / half)
    ang = jnp.arange(l, dtype=F32)[:, None] * inv_freq[None, :]
    cos, sin = jnp.cos(ang), jnp.sin(ang)
    return jnp.concatenate([cos, cos], axis=-1), jnp.concatenate([-sin, sin], axis=-1)


def _retention(proj, decay_exp, tok_heads):
    b, l, _ = proj.shape
    assert l % RET_CHUNK == 0 and (l // RET_CHUNK) % min(RET_UNROLL, l // RET_CHUNK) == 0
    dec = jnp.broadcast_to(decay_exp.astype(F32).T[:, :, None], (tok_heads, 2, HEAD_DIM))
    nb = _sequences_per_step(b, l)
    head_spec = lambda part: pl.BlockSpec((nb, l, HEAD_DIM), lambda bi, h: (bi, 0, part * tok_heads + h))
    n_chunks = l // RET_CHUNK
    return pl.pallas_call(
        functools.partial(_retention_kernel, n_chunks=n_chunks),
        grid=(b // nb, tok_heads),
        in_specs=[pl.BlockSpec((None, 2, HEAD_DIM), lambda bi, h: (h, 0, 0)),
                  head_spec(0), head_spec(1), head_spec(2), head_spec(3)],
        out_specs=pl.BlockSpec((nb, l, HEAD_DIM), lambda bi, h: (bi, 0, h)),
        out_shape=jax.ShapeDtypeStruct((b, l, tok_heads * HEAD_DIM), BF16),
        scratch_shapes=[pltpu.VMEM((n_chunks, HEAD_DIM, HEAD_DIM), F32),
                        pltpu.VMEM((n_chunks, HEAD_DIM, HEAD_DIM), BF16)],
        compiler_params=_params("parallel", "parallel"),
        name="retention",
    )(dec, proj, proj, proj, proj)


def _na_kernel(bias_ref, q_ref, k_ref, v_ref, o_ref, *, rows):
    win = NA_KH * GRID_W
    unroll = min(NA_UNROLL, rows)

    def one_sequence(q, k, v, o):
        def rows_step(i, carry):
            qrows = [i * unroll + u for u in range(unroll)]
            starts = [jnp.clip(r - NA_KH // 2, 0, rows - NA_KH) for r in qrows]
            q_offs = [pl.multiple_of(r * GRID_W, GRID_W) for r in qrows]
            k_offs = [pl.multiple_of(rs * GRID_W, GRID_W) for rs in starts]
            scores = [lax.dot_general(q[pl.ds(q_off, GRID_W), :], k[pl.ds(k_off, win), :], NT_DIMS,
                                      preferred_element_type=F32)
                      for q_off, k_off in zip(q_offs, k_offs)]
            probs, denoms = [], []
            for r, rs, s in zip(qrows, starts, scores):
                s = s + bias_ref[rs - r + NA_KH - 1]
                p = jnp.exp2(s - jnp.max(s, axis=-1, keepdims=True))
                denoms.append(jnp.sum(p, axis=-1, keepdims=True))
                probs.append(p.astype(BF16))
            outs = [jnp.dot(p, v[pl.ds(k_off, win), :], preferred_element_type=F32)
                    for p, k_off in zip(probs, k_offs)]
            for q_off, out, denom in zip(q_offs, outs, denoms):
                o[pl.ds(q_off, GRID_W), :] = (out / denom).astype(o.dtype)
            return carry

        lax.fori_loop(0, rows // unroll, rows_step, 0)

    def sequence_step(s, carry):
        one_sequence(q_ref.at[s], k_ref.at[s], v_ref.at[s], o_ref.at[s])
        return carry

    lax.fori_loop(0, q_ref.shape[0], sequence_step, 0)


def _na_bias_table(rpb):
    c = np.arange(GRID_W)
    win_start = np.clip(c - NA_KW // 2, 0, GRID_W - NA_KW)
    valid = (c[None, :] >= win_start[:, None]) & (c[None, :] < win_start[:, None] + NA_KW)
    dc_idx = np.clip(c[None, :] - c[:, None] + NA_KW - 1, 0, 2 * NA_KW - 2)
    toeplitz = jnp.where(valid[None, None], rpb.astype(F32)[:, :, dc_idx] * LOG2_E, -jnp.inf)
    table = jnp.stack([toeplitz[:, o:o + NA_KH] for o in range(NA_KH)], axis=1)
    table = table.transpose(0, 1, 3, 2, 4)
    return table.reshape(rpb.shape[0], NA_KH, GRID_W, NA_KH * GRID_W)


def _neighbourhood(proj, rpb, tok_heads):
    b, l, _ = proj.shape
    rows = l // GRID_W
    assert l % GRID_W == 0 and rows >= NA_KH and rows % min(NA_UNROLL, rows) == 0
    bias = _na_bias_table(rpb)
    nb = _sequences_per_step(b, l)
    head_spec = lambda part: pl.BlockSpec((nb, l, HEAD_DIM), lambda bi, h: (bi, 0, part * tok_heads + h))
    return pl.pallas_call(
        functools.partial(_na_kernel, rows=rows),
        grid=(b // nb, tok_heads),
        in_specs=[pl.BlockSpec((None, NA_KH, GRID_W, NA_KH * GRID_W), lambda bi, h: (h, 0, 0, 0)),
                  head_spec(0), head_spec(1), head_spec(2)],
        out_specs=pl.BlockSpec((nb, l, HEAD_DIM), lambda bi, h: (bi, 0, h)),
        out_shape=jax.ShapeDtypeStruct((b, l, tok_heads * HEAD_DIM), BF16),
        compiler_params=_params("parallel", "parallel"),
        name="neighbourhood",
    )(bias, proj, proj, proj)


def _trunk(x, mem, norm_gain, mem_norm_gain, w_mem_kv, w_out, w_mlp_in, w_mlp_out, w_in_ret, ret_decay,
           w_in_na, na_rpb):
    b, l, d = x.shape
    n_mem = mem.shape[1]
    tok_width = w_out.shape[1] - MEM_WIDTH
    tok_heads = tok_width // HEAD_DIM
    xf = x.reshape(b * l, d)
    memf = mem.reshape(b * n_mem, d)
    for i in range(norm_gain.shape[0]):
        g = norm_gain[i]
        mem_kv = _norm_matmul(memf, mem_norm_gain[i], w_mem_kv, i).reshape(b, n_mem, 2 * MEM_WIDTH)
        j = i // N_MIXERS
        if i % N_MIXERS == 0:
            proj = _norm_matmul(xf, g[0], w_in_ret, j, columns="retention", tok_width=tok_width,
                                seq=l).reshape(b, l, -1)
            tok = _retention(proj, ret_decay[j], tok_heads)
        else:
            proj = _norm_matmul(xf, g[0], w_in_na, j, columns="neighbourhood", tok_width=tok_width,
                                seq=l).reshape(b, l, -1)
            tok = _neighbourhood(proj, na_rpb[j], tok_heads)
        x1 = _out_proj(tok.reshape(b * l, tok_width), proj.reshape(b * l, -1), mem_kv, xf, g[1], w_out, i)
        xf = _mlp(x1, g[2], g[3], w_mlp_in, w_mlp_out, i)
    return xf.reshape(b, l, d)


def kernel(x_prompt, x_sample, mem_prompt, mem_sample, norm_gain, mem_norm_gain, w_mem_kv, w_out, w_mlp_in,
           w_mlp_out, w_in_ret, ret_decay, w_in_na, na_rpb):
    weights = [w.astype(BF16) for w in (w_mem_kv, w_out, w_mlp_in, w_mlp_out, w_in_ret)]
    w_mem_kv, w_out, w_mlp_in, w_mlp_out, w_in_ret = weights
    w_in_na = w_in_na.astype(BF16)
    run = functools.partial(_trunk, norm_gain=norm_gain, mem_norm_gain=mem_norm_gain, w_mem_kv=w_mem_kv,
                            w_out=w_out, w_mlp_in=w_mlp_in, w_mlp_out=w_mlp_out, w_in_ret=w_in_ret,
                            ret_decay=ret_decay, w_in_na=w_in_na, na_rpb=na_rpb)
    return (run(x_prompt, mem_prompt), run(x_sample, mem_sample))
```

```python
import functools

import jax
import jax.numpy as jnp
import numpy as np
from jax import lax
from jax.experimental import pallas as pl
from jax.experimental.pallas import tpu as pltpu

HEAD_DIM = 128
MEM_HEADS = 4
MEM_WIDTH = MEM_HEADS * HEAD_DIM
N_MIXERS = 2
RET_CHUNK = 128
ROPE_BASE = 10000.0
GRID_W = 64
NA_KH = 8
NA_KW = 16
NORM_EPS = 1e-6
ATTN_SCALE = HEAD_DIM ** -0.5
LOG2_E = 1.4426950408889634
MLP_HIDDEN_TILE = 1024
MIXER_STEP_TOKENS = 8192
RET_UNROLL = 64
NA_UNROLL = 64

V7X_VMEM_BYTES = 64 * 1024 * 1024
VMEM_LIMIT_BYTES = V7X_VMEM_BYTES - 4 * 1024 * 1024

F32 = jnp.float32
BF16 = jnp.bfloat16
NT_DIMS = (((1,), (1,)), ((), ()))


def _params(*semantics):
    return pltpu.CompilerParams(dimension_semantics=semantics, vmem_limit_bytes=VMEM_LIMIT_BYTES)


def _rms(x, gain):
    return x * lax.rsqrt(jnp.mean(x * x, axis=-1, keepdims=True) + NORM_EPS) * gain


def _rotate(x, cos2, sin2):
    return x * cos2 + pltpu.roll(x, HEAD_DIM // 2, axis=1) * sin2


def _retention_columns(res, cos2, sin2, tok_width):
    heads = tok_width // HEAD_DIM
    blocks = []
    for c in range(res.shape[1] // HEAD_DIM):
        blk = res[:, c * HEAD_DIM:(c + 1) * HEAD_DIM]
        part = c // heads
        if part == 0:
            blk = _rotate(blk, cos2, sin2)
        elif part == 1:
            blk = _rotate(blk, cos2, sin2) * ATTN_SCALE
        elif part == 3:
            blk = blk * (1.0 / (1.0 + jnp.exp(-blk)))
        blocks.append(blk.astype(BF16))
    return jnp.concatenate(blocks, axis=1)


def _neighbourhood_columns(res, tok_width):
    q = res[:, :tok_width] * (ATTN_SCALE * LOG2_E)
    return jnp.concatenate([q.astype(BF16), res[:, tok_width:].astype(BF16)], axis=1)


def _norm_matmul_kernel(x_ref, g_ref, w_ref, *rest, sub_rows, columns, tok_width):
    o_ref = rest[-1]
    gain = g_ref[...]
    for r in range(x_ref.shape[0] // sub_rows):
        rows = slice(r * sub_rows, (r + 1) * sub_rows)
        h = _rms(x_ref[rows, :], gain).astype(BF16)
        res = jnp.dot(h, w_ref[...], preferred_element_type=F32)
        if columns == "retention":
            cos_ref, sin_ref = rest[0], rest[1]
            res = _retention_columns(res, cos_ref[rows, :], sin_ref[rows, :], tok_width)
        elif columns == "neighbourhood":
            res = _neighbourhood_columns(res, tok_width)
        o_ref[rows, :] = res.astype(o_ref.dtype)


def _norm_matmul(x, gain, w, layer, *, columns=None, tok_width=None, seq=None, tm=512, sub_rows=256):
    t, d = x.shape
    n = w.shape[2]
    tm = min(tm, t if seq is None else seq)
    sub_rows = min(sub_rows, tm)
    assert t % tm == 0 and tm % sub_rows == 0
    operands = [x, gain.reshape(1, d), w]
    in_specs = [
        pl.BlockSpec((tm, d), lambda i: (i, 0)),
        pl.BlockSpec((1, d), lambda i: (0, 0)),
        pl.BlockSpec((None, d, n), lambda i: (layer, 0, 0), pipeline_mode=pl.Buffered(1)),
    ]
    if columns == "retention":
        assert seq % tm == 0
        tiles_per_seq = seq // tm
        operands += list(_rotary_tables(seq))
        in_specs += [pl.BlockSpec((tm, HEAD_DIM), lambda i: (i % tiles_per_seq, 0)),
                     pl.BlockSpec((tm, HEAD_DIM), lambda i: (i % tiles_per_seq, 0))]
    return pl.pallas_call(
        functools.partial(_norm_matmul_kernel, sub_rows=sub_rows, columns=columns, tok_width=tok_width),
        grid=(t // tm,),
        in_specs=in_specs,
        out_specs=pl.BlockSpec((tm, n), lambda i: (i, 0)),
        out_shape=jax.ShapeDtypeStruct((t, n), BF16),
        compiler_params=_params("parallel"),
        name="norm_matmul",
    )(*operands)


def _mem_scores(q, k_ref):
    return [lax.dot_general(q[:, h * HEAD_DIM:(h + 1) * HEAD_DIM], k_ref[:, h * HEAD_DIM:(h + 1) * HEAD_DIM],
                            NT_DIMS, preferred_element_type=F32) for h in range(MEM_HEADS)]


def _mem_values(scores, v_ref):
    probs, denoms = [], []
    for s in scores:
        s = s * ATTN_SCALE
        p = jnp.exp(s - jnp.max(s, axis=-1, keepdims=True))
        denoms.append(jnp.sum(p, axis=-1, keepdims=True))
        probs.append(p.astype(BF16))
    outs = [jnp.dot(p, v_ref[:, h * HEAD_DIM:(h + 1) * HEAD_DIM], preferred_element_type=F32)
            for h, p in enumerate(probs)]
    return jnp.concatenate([(o / dn).astype(BF16) for o, dn in zip(outs, denoms)], axis=1)


def _out_proj_kernel(tok_ref, qm_ref, k_ref, v_ref, x_ref, g_ref, wt_ref, wm_ref, o_ref, *, sub_rows):
    gain = g_ref[...]
    n_sub = x_ref.shape[0] // sub_rows
    rows = [slice(r * sub_rows, (r + 1) * sub_rows) for r in range(n_sub)]
    mem = _mem_values(_mem_scores(qm_ref[rows[0], :], k_ref), v_ref)
    for r in range(n_sub):
        if r + 1 < n_sub:
            next_scores = _mem_scores(qm_ref[rows[r + 1], :], k_ref)
        mixed = (jnp.dot(tok_ref[rows[r], :], wt_ref[...], preferred_element_type=F32)
                 + jnp.dot(mem, wm_ref[...], preferred_element_type=F32))
        if r + 1 < n_sub:
            mem = _mem_values(next_scores, v_ref)
        o_ref[rows[r], :] = x_ref[rows[r], :] + _rms(mixed, gain)


def _out_proj(tok, proj, mem_kv, x, gain, w_out, layer, *, tm=1024, sub_rows=256):
    t, d = x.shape
    b, m, _ = mem_kv.shape
    seq = t // b
    tm = min(tm, seq)
    sub_rows = min(sub_rows, tm)
    tw, n = tok.shape[1], proj.shape[1]
    assert seq % tm == 0 and tm % sub_rows == 0 and tw % MEM_WIDTH == 0 and n % MEM_WIDTH == 0
    assert w_out.shape[1] == tw + MEM_WIDTH
    tiles_per_seq = seq // tm
    q_block = n // MEM_WIDTH - 1
    return pl.pallas_call(
        functools.partial(_out_proj_kernel, sub_rows=sub_rows),
        grid=(t // tm,),
        in_specs=[
            pl.BlockSpec((tm, tw), lambda i: (i, 0)),
            pl.BlockSpec((tm, MEM_WIDTH), lambda i: (i, q_block)),
            pl.BlockSpec((None, m, MEM_WIDTH), lambda i: (i // tiles_per_seq, 0, 0)),
            pl.BlockSpec((None, m, MEM_WIDTH), lambda i: (i // tiles_per_seq, 0, 1)),
            pl.BlockSpec((tm, d), lambda i: (i, 0)),
            pl.BlockSpec((1, d), lambda i: (0, 0)),
            pl.BlockSpec((None, tw, d), lambda i: (layer, 0, 0), pipeline_mode=pl.Buffered(1)),
            pl.BlockSpec((None, MEM_WIDTH, d), lambda i: (layer, tw // MEM_WIDTH, 0),
                         pipeline_mode=pl.Buffered(1)),
        ],
        out_specs=pl.BlockSpec((tm, d), lambda i: (i, 0)),
        out_shape=jax.ShapeDtypeStruct((t, d), F32),
        compiler_params=_params("parallel"),
        name="out_proj",
    )(tok, proj, mem_kv, mem_kv, x, gain.reshape(1, d), w_out, w_out)


def _mlp_kernel(x_ref, gi_ref, go_ref, w1_ref, w2_ref, o_ref, h_ref, *, sub_rows):
    j = pl.program_id(1)
    last = pl.num_programs(1) - 1
    n_sub = x_ref.shape[0] // sub_rows

    def step(first, final):
        hidden = []
        for r in range(n_sub):
            rows = slice(r * sub_rows, (r + 1) * sub_rows)
            if first:
                h = _rms(x_ref[rows, :], gi_ref[...]).astype(BF16)
                h_ref[rows, :] = h
            else:
                h = h_ref[rows, :]
            u = jnp.dot(h, w1_ref[...], preferred_element_type=F32)
            hidden.append(jnp.square(jnp.maximum(u, 0.0)).astype(BF16))
        for r in range(n_sub):
            rows = slice(r * sub_rows, (r + 1) * sub_rows)
            acc = jnp.dot(hidden[r], w2_ref[...], preferred_element_type=F32)
            if not first:
                acc = o_ref[rows, :] + acc
            if final:
                acc = x_ref[rows, :] + _rms(acc, go_ref[...])
            o_ref[rows, :] = acc

    pl.when(j == 0)(lambda: step(True, False))
    pl.when(jnp.logical_and(j > 0, j < last))(lambda: step(False, False))
    pl.when(j == last)(lambda: step(False, True))


def _mlp(x, gain_in, gain_out, w1, w2, layer, *, tm=1024, tf=MLP_HIDDEN_TILE, sub_rows=512):
    t, d = x.shape
    f = w1.shape[2]
    tm = min(tm, t)
    sub_rows = min(sub_rows, tm)
    assert t % tm == 0 and tm % sub_rows == 0 and f % tf == 0 and f // tf >= 2
    return pl.pallas_call(
        functools.partial(_mlp_kernel, sub_rows=sub_rows),
        grid=(t // tm, f // tf),
        in_specs=[
            pl.BlockSpec((tm, d), lambda i, j: (i, 0)),
            pl.BlockSpec((1, d), lambda i, j: (0, 0)),
            pl.BlockSpec((1, d), lambda i, j: (0, 0)),
            pl.BlockSpec((None, d, tf), lambda i, j: (layer, 0, j)),
            pl.BlockSpec((None, tf, d), lambda i, j: (layer, j, 0)),
        ],
        out_specs=pl.BlockSpec((tm, d), lambda i, j: (i, 0)),
        out_shape=jax.ShapeDtypeStruct((t, d), F32),
        scratch_shapes=[pltpu.VMEM((tm, d), BF16)],
        compiler_params=_params("parallel", "arbitrary"),
        name="mlp",
    )(x, gain_in.reshape(1, d), gain_out.reshape(1, d), w1, w2)


def _sequences_per_step(b, l):
    nb = max(1, min(b, MIXER_STEP_TOKENS // l))
    while b % nb:
        nb -= 1
    return nb


def _retention_kernel(dec_ref, q_ref, k_ref, v_ref, g_ref, o_ref, kvb_ref, sf_ref, *, n_chunks):
    c = RET_CHUNK
    unroll = min(RET_UNROLL, n_chunks)
    log_gamma = jnp.log1p(-jnp.exp2(-dec_ref[...]))
    lgf, lgb = log_gamma[0:1, :], log_gamma[1:2, :]
    lane = lax.broadcasted_iota(jnp.int32, (1, c), 1).astype(F32)
    k_decay_f_row = jnp.exp((c - 1.0 - lane) * lgf)
    k_decay_b_row = jnp.exp(lane * lgb)
    chunk_decay_f = jnp.exp(c * lgf)
    chunk_decay_b = jnp.exp(c * lgb)
    row = lax.broadcasted_iota(jnp.int32, (c, c), 0).astype(F32)
    col = lax.broadcasted_iota(jnp.int32, (c, c), 1).astype(F32)
    diff = row - col
    decay_intra = jnp.exp(jnp.where(diff >= 0, diff * lgf, -diff * lgb))
    pos = lax.broadcasted_iota(jnp.int32, (c, HEAD_DIM), 0).astype(F32)
    q_decay_f = jnp.exp((pos + 1.0) * lgf)
    q_decay_b = jnp.exp((c - pos) * lgb)
    zero_state = jnp.zeros((HEAD_DIM, HEAD_DIM), F32)

    def one_sequence(q, k, v, g, o):
        def forward_group(i, sf):
            chunks = [i * unroll + u for u in range(unroll)]
            offs = [pl.multiple_of(n * c, c) for n in chunks]
            lhs = []
            for off in offs:
                kr_t = k[pl.ds(off, c), :].astype(F32).T
                lhs.append(jnp.concatenate([kr_t * k_decay_f_row, kr_t * k_decay_b_row], axis=0).astype(BF16))
            kvs = [jnp.dot(lhs[u], v[pl.ds(off, c), :], preferred_element_type=F32)
                   for u, off in enumerate(offs)]
            for n, kv in zip(chunks, kvs):
                sf_ref[n] = sf.astype(BF16)
                sf = chunk_decay_f * sf + kv[0:HEAD_DIM, :]
                kvb_ref[n] = kv[HEAD_DIM:2 * HEAD_DIM, :]
            return sf

        lax.fori_loop(0, n_chunks // unroll, forward_group, zero_state)

        def backward_group(t, sb):
            i = n_chunks // unroll - 1 - t
            chunks = [i * unroll + u for u in range(unroll)]
            offs = [pl.multiple_of(n * c, c) for n in chunks]
            scores = [lax.dot_general(q[pl.ds(off, c), :], k[pl.ds(off, c), :], NT_DIMS,
                                      preferred_element_type=F32) for off in offs]
            states = [None] * unroll
            for u in reversed(range(unroll)):
                states[u] = sb.astype(BF16)
                sb = chunk_decay_b * sb + kvb_ref[chunks[u]]
            ys = []
            for n, off, s, sb_n in zip(chunks, offs, scores, states):
                qr = q[pl.ds(off, c), :].astype(F32)
                lhs = jnp.concatenate([s * decay_intra, qr * q_decay_f, qr * q_decay_b], axis=1).astype(BF16)
                rhs = jnp.concatenate([v[pl.ds(off, c), :], sf_ref[n], sb_n], axis=0)
                ys.append(jnp.dot(lhs, rhs, preferred_element_type=F32))
            for off, y in zip(offs, ys):
                y = y * lax.rsqrt(jnp.mean(y * y, axis=-1, keepdims=True) + NORM_EPS)
                o[pl.ds(off, c), :] = (g[pl.ds(off, c), :].astype(F32) * y).astype(o.dtype)
            return sb

        lax.fori_loop(0, n_chunks // unroll, backward_group, zero_state)

    def sequence_step(s, carry):
        one_sequence(q_ref.at[s], k_ref.at[s], v_ref.at[s], g_ref.at[s], o_ref.at[s])
        return carry

    lax.fori_loop(0, q_ref.shape[0], sequence_step, 0)


def _rotary_tables(l):
    half = HEAD_DIM // 2
    inv_freq = ROPE_BASE ** (-jnp.arange(half, dtype=F32) / half)
    ang = jnp.arange(l, dtype=F32)[:, None] * inv_freq[None, :]
    cos, sin = jnp.cos(ang), jnp.sin(ang)
    return jnp.concatenate([cos, cos], axis=-1), jnp.concatenate([-sin, sin], axis=-1)


def _retention(proj, decay_exp, tok_heads):
    b, l, _ = proj.shape
    assert l % RET_CHUNK == 0 and (l // RET_CHUNK) % min(RET_UNROLL, l // RET_CHUNK) == 0
    dec = jnp.broadcast_to(decay_exp.astype(F32).T[:, :, None], (tok_heads, 2, HEAD_DIM))
    nb = _sequences_per_step(b, l)
    head_spec = lambda part: pl.BlockSpec((nb, l, HEAD_DIM), lambda bi, h: (bi, 0, part * tok_heads + h))
    n_chunks = l // RET_CHUNK
    return pl.pallas_call(
        functools.partial(_retention_kernel, n_chunks=n_chunks),
        grid=(b // nb, tok_heads),
        in_specs=[pl.BlockSpec((None, 2, HEAD_DIM), lambda bi, h: (h, 0, 0)),
                  head_spec(0), head_spec(1), head_spec(2), head_spec(3)],
        out_specs=pl.BlockSpec((nb, l, HEAD_DIM), lambda bi, h: (bi, 0, h)),
        out_shape=jax.ShapeDtypeStruct((b, l, tok_heads * HEAD_DIM), BF16),
        scratch_shapes=[pltpu.VMEM((n_chunks, HEAD_DIM, HEAD_DIM), F32),
                        pltpu.VMEM((n_chunks, HEAD_DIM, HEAD_DIM), BF16)],
        compiler_params=_params("parallel", "parallel"),
        name="retention",
    )(dec, proj, proj, proj, proj)


def _na_kernel(bias_ref, q_ref, k_ref, v_ref, o_ref, *, rows):
    win = NA_KH * GRID_W
    unroll = min(NA_UNROLL, rows)

    def one_sequence(q, k, v, o):
        def rows_step(i, carry):
            qrows = [i * unroll + u for u in range(unroll)]
            starts = [jnp.clip(r - NA_KH // 2, 0, rows - NA_KH) for r in qrows]
            q_offs = [pl.multiple_of(r * GRID_W, GRID_W) for r in qrows]
            k_offs = [pl.multiple_of(rs * GRID_W, GRID_W) for rs in starts]
            scores = [lax.dot_general(q[pl.ds(q_off, GRID_W), :], k[pl.ds(k_off, win), :], NT_DIMS,
                                      preferred_element_type=F32)
                      for q_off, k_off in zip(q_offs, k_offs)]
            probs, denoms = [], []
            for r, rs, s in zip(qrows, starts, scores):
                s = s + bias_ref[rs - r + NA_KH - 1]
                p = jnp.exp2(s - jnp.max(s, axis=-1, keepdims=True))
                denoms.append(jnp.sum(p, axis=-1, keepdims=True))
                probs.append(p.astype(BF16))
            outs = [jnp.dot(p, v[pl.ds(k_off, win), :], preferred_element_type=F32)
                    for p, k_off in zip(probs, k_offs)]
            for q_off, out, denom in zip(q_offs, outs, denoms):
                o[pl.ds(q_off, GRID_W), :] = (out / denom).astype(o.dtype)
            return carry

        lax.fori_loop(0, rows // unroll, rows_step, 0)

    def sequence_step(s, carry):
        one_sequence(q_ref.at[s], k_ref.at[s], v_ref.at[s], o_ref.at[s])
        return carry

    lax.fori_loop(0, q_ref.shape[0], sequence_step, 0)


def _na_bias_table(rpb):
    c = np.arange(GRID_W)
    win_start = np.clip(c - NA_KW // 2, 0, GRID_W - NA_KW)
    valid = (c[None, :] >= win_start[:, None]) & (c[None, :] < win_start[:, None] + NA_KW)
    dc_idx = np.clip(c[None, :] - c[:, None] + NA_KW - 1, 0, 2 * NA_KW - 2)
    toeplitz = jnp.where(valid[None, None], rpb.astype(F32)[:, :, dc_idx] * LOG2_E, -jnp.inf)
    table = jnp.stack([toeplitz[:, o:o + NA_KH] for o in range(NA_KH)], axis=1)
    table = table.transpose(0, 1, 3, 2, 4)
    return table.reshape(rpb.shape[0], NA_KH, GRID_W, NA_KH * GRID_W)


def _neighbourhood(proj, rpb, tok_heads):
    b, l, _ = proj.shape
    rows = l // GRID_W
    assert l % GRID_W == 0 and rows >= NA_KH and rows % min(NA_UNROLL, rows) == 0
    bias = _na_bias_table(rpb)
    nb = _sequences_per_step(b, l)
    head_spec = lambda part: pl.BlockSpec((nb, l, HEAD_DIM), lambda bi, h: (bi, 0, part * tok_heads + h))
    return pl.pallas_call(
        functools.partial(_na_kernel, rows=rows),
        grid=(b // nb, tok_heads),
        in_specs=[pl.BlockSpec((None, NA_KH, GRID_W, NA_KH * GRID_W), lambda bi, h: (h, 0, 0, 0)),
                  head_spec(0), head_spec(1), head_spec(2)],
        out_specs=pl.BlockSpec((nb, l, HEAD_DIM), lambda bi, h: (bi, 0, h)),
        out_shape=jax.ShapeDtypeStruct((b, l, tok_heads * HEAD_DIM), BF16),
        compiler_params=_params("parallel", "parallel"),
        name="neighbourhood",
    )(bias, proj, proj, proj)


def _trunk(x, mem, norm_gain, mem_norm_gain, w_mem_kv, w_out, w_mlp_in, w_mlp_out, w_in_ret, ret_decay,
           w_in_na, na_rpb):
    b, l, d = x.shape
    n_mem = mem.shape[1]
    tok_width = w_out.shape[1] - MEM_WIDTH
    tok_heads = tok_width // HEAD_DIM
    xf = x.reshape(b * l, d)
    memf = mem.reshape(b * n_mem, d)
    for i in range(norm_gain.shape[0]):
        g = norm_gain[i]
        mem_kv = _norm_matmul(memf, mem_norm_gain[i], w_mem_kv, i).reshape(b, n_mem, 2 * MEM_WIDTH)
        j = i // N_MIXERS
        if i % N_MIXERS == 0:
            proj = _norm_matmul(xf, g[0], w_in_ret, j, columns="retention", tok_width=tok_width,
                                seq=l).reshape(b, l, -1)
            tok = _retention(proj, ret_decay[j], tok_heads)
        else:
            proj = _norm_matmul(xf, g[0], w_in_na, j, columns="neighbourhood", tok_width=tok_width,
                                seq=l).reshape(b, l, -1)
            tok = _neighbourhood(proj, na_rpb[j], tok_heads)
        x1 = _out_proj(tok.reshape(b * l, tok_width), proj.reshape(b * l, -1), mem_kv, xf, g[1], w_out, i)
        xf = _mlp(x1, g[2], g[3], w_mlp_in, w_mlp_out, i)
    return xf.reshape(b, l, d)


def kernel(x_prompt, x_sample, mem_prompt, mem_sample, norm_gain, mem_norm_gain, w_mem_kv, w_out, w_mlp_in,
           w_mlp_out, w_in_ret, ret_decay, w_in_na, na_rpb):
    weights = [w.astype(BF16) for w in (w_mem_kv, w_out, w_mlp_in, w_mlp_out, w_in_ret)]
    w_mem_kv, w_out, w_mlp_in, w_mlp_out, w_in_ret = weights
    w_in_na = w_in_na.astype(BF16)
    run = functools.partial(_trunk, norm_gain=norm_gain, mem_norm_gain=mem_norm_gain, w_mem_kv=w_mem_kv,
                            w_out=w_out, w_mlp_in=w_mlp_in, w_mlp_out=w_mlp_out, w_in_ret=w_in_ret,
                            ret_decay=ret_decay, w_in_na=w_in_na, na_rpb=na_rpb)
    return (run(x_prompt, mem_prompt), run(x_sample, mem_sample))
```

```python
import functools

import jax
import jax.numpy as jnp
import numpy as np
from jax import lax
from jax.experimental import pallas as pl
from jax.experimental.pallas import tpu as pltpu

HEAD_DIM = 128
MEM_HEADS = 4
MEM_WIDTH = MEM_HEADS * HEAD_DIM
N_MIXERS = 2
RET_CHUNK = 128
ROPE_BASE = 10000.0
GRID_W = 64
NA_KH = 8
NA_KW = 16
NORM_EPS = 1e-6
ATTN_SCALE = HEAD_DIM ** -0.5
LOG2_E = 1.4426950408889634
MLP_HIDDEN_TILE = 1024
MIXER_STEP_TOKENS = 8192
RET_UNROLL = 64
NA_UNROLL = 64

V7X_VMEM_BYTES = 64 * 1024 * 1024
VMEM_LIMIT_BYTES = V7X_VMEM_BYTES - 4 * 1024 * 1024

F32 = jnp.float32
BF16 = jnp.bfloat16
NT_DIMS = (((1,), (1,)), ((), ()))


def _params(*semantics):
    return pltpu.CompilerParams(dimension_semantics=semantics, vmem_limit_bytes=VMEM_LIMIT_BYTES)


def _rms(x, gain):
    return x * lax.rsqrt(jnp.mean(x * x, axis=-1, keepdims=True) + NORM_EPS) * gain


def _rotate(x, cos2, sin2):
    return x * cos2 + pltpu.roll(x, HEAD_DIM // 2, axis=1) * sin2


def _retention_columns(res, cos2, sin2, tok_width):
    heads = tok_width // HEAD_DIM
    blocks = []
    for c in range(res.shape[1] // HEAD_DIM):
        blk = res[:, c * HEAD_DIM:(c + 1) * HEAD_DIM]
        part = c // heads
        if part == 0:
            blk = _rotate(blk, cos2, sin2)
        elif part == 1:
            blk = _rotate(blk, cos2, sin2) * ATTN_SCALE
        elif part == 3:
            blk = blk * (1.0 / (1.0 + jnp.exp(-blk)))
        blocks.append(blk.astype(BF16))
    return jnp.concatenate(blocks, axis=1)


def _neighbourhood_columns(res, tok_width):
    q = res[:, :tok_width] * (ATTN_SCALE * LOG2_E)
    return jnp.concatenate([q.astype(BF16), res[:, tok_width:].astype(BF16)], axis=1)


def _norm_matmul_kernel(x_ref, g_ref, w_ref, *rest, sub_rows, columns, tok_width):
    o_ref = rest[-1]
    gain = g_ref[...]
    for r in range(x_ref.shape[0] // sub_rows):
        rows = slice(r * sub_rows, (r + 1) * sub_rows)
        h = _rms(x_ref[rows, :], gain).astype(BF16)
        res = jnp.dot(h, w_ref[...], preferred_element_type=F32)
        if columns == "retention":
            cos_ref, sin_ref = rest[0], rest[1]
            res = _retention_columns(res, cos_ref[rows, :], sin_ref[rows, :], tok_width)
        elif columns == "neighbourhood":
            res = _neighbourhood_columns(res, tok_width)
        o_ref[rows, :] = res.astype(o_ref.dtype)


def _norm_matmul(x, gain, w, layer, *, columns=None, tok_width=None, seq=None, tm=512, sub_rows=256):
    t, d = x.shape
    n = w.shape[2]
    tm = min(tm, t if seq is None else seq)
    sub_rows = min(sub_rows, tm)
    assert t % tm == 0 and tm % sub_rows == 0
    operands = [x, gain.reshape(1, d), w]
    in_specs = [
        pl.BlockSpec((tm, d), lambda i: (i, 0)),
        pl.BlockSpec((1, d), lambda i: (0, 0)),
        pl.BlockSpec((None, d, n), lambda i: (layer, 0, 0), pipeline_mode=pl.Buffered(1)),
    ]
    if columns == "retention":
        assert seq % tm == 0
        tiles_per_seq = seq // tm
        operands += list(_rotary_tables(seq))
        in_specs += [pl.BlockSpec((tm, HEAD_DIM), lambda i: (i % tiles_per_seq, 0)),
                     pl.BlockSpec((tm, HEAD_DIM), lambda i: (i % tiles_per_seq, 0))]
    return pl.pallas_call(
        functools.partial(_norm_matmul_kernel, sub_rows=sub_rows, columns=columns, tok_width=tok_width),
        grid=(t // tm,),
        in_specs=in_specs,
        out_specs=pl.BlockSpec((tm, n), lambda i: (i, 0)),
        out_shape=jax.ShapeDtypeStruct((t, n), BF16),
        compiler_params=_params("parallel"),
        name="norm_matmul",
    )(*operands)


def _mem_scores(q, k_ref):
    return [lax.dot_general(q[:, h * HEAD_DIM:(h + 1) * HEAD_DIM], k_ref[:, h * HEAD_DIM:(h + 1) * HEAD_DIM],
                            NT_DIMS, preferred_element_type=F32) for h in range(MEM_HEADS)]


def _mem_values(scores, v_ref):
    probs, denoms = [], []
    for s in scores:
        s = s * ATTN_SCALE
        p = jnp.exp(s - jnp.max(s, axis=-1, keepdims=True))
        denoms.append(jnp.sum(p, axis=-1, keepdims=True))
        probs.append(p.astype(BF16))
    outs = [jnp.dot(p, v_ref[:, h * HEAD_DIM:(h + 1) * HEAD_DIM], preferred_element_type=F32)
            for h, p in enumerate(probs)]
    return jnp.concatenate([(o / dn).astype(BF16) for o, dn in zip(outs, denoms)], axis=1)


def _out_proj_kernel(tok_ref, qm_ref, k_ref, v_ref, x_ref, g_ref, wt_ref, wm_ref, o_ref, *, sub_rows):
    gain = g_ref[...]
    n_sub = x_ref.shape[0] // sub_rows
    rows = [slice(r * sub_rows, (r + 1) * sub_rows) for r in range(n_sub)]
    mem = _mem_values(_mem_scores(qm_ref[rows[0], :], k_ref), v_ref)
    for r in range(n_sub):
        if r + 1 < n_sub:
            next_scores = _mem_scores(qm_ref[rows[r + 1], :], k_ref)
        mixed = (jnp.dot(tok_ref[rows[r], :], wt_ref[...], preferred_element_type=F32)
                 + jnp.dot(mem, wm_ref[...], preferred_element_type=F32))
        if r + 1 < n_sub:
            mem = _mem_values(next_scores, v_ref)
        o_ref[rows[r], :] = x_ref[rows[r], :] + _rms(mixed, gain)


def _out_proj(tok, proj, mem_kv, x, gain, w_out, layer, *, tm=1024, sub_rows=256):
    t, d = x.shape
    b, m, _ = mem_kv.shape
    seq = t // b
    tm = min(tm, seq)
    sub_rows = min(sub_rows, tm)
    tw, n = tok.shape[1], proj.shape[1]
    assert seq % tm == 0 and tm % sub_rows == 0 and tw % MEM_WIDTH == 0 and n % MEM_WIDTH == 0
    assert w_out.shape[1] == tw + MEM_WIDTH
    tiles_per_seq = seq // tm
    q_block = n // MEM_WIDTH - 1
    return pl.pallas_call(
        functools.partial(_out_proj_kernel, sub_rows=sub_rows),
        grid=(t // tm,),
        in_specs=[
            pl.BlockSpec((tm, tw), lambda i: (i, 0)),
            pl.BlockSpec((tm, MEM_WIDTH), lambda i: (i, q_block)),
            pl.BlockSpec((None, m, MEM_WIDTH), lambda i: (i // tiles_per_seq, 0, 0)),
            pl.BlockSpec((None, m, MEM_WIDTH), lambda i: (i // tiles_per_seq, 0, 1)),
            pl.BlockSpec((tm, d), lambda i: (i, 0)),
            pl.BlockSpec((1, d), lambda i: (0, 0)),
            pl.BlockSpec((None, tw, d), lambda i: (layer, 0, 0), pipeline_mode=pl.Buffered(1)),
            pl.BlockSpec((None, MEM_WIDTH, d), lambda i: (layer, tw // MEM_WIDTH, 0),
                         pipeline_mode=pl.Buffered(1)),
        ],
        out_specs=pl.BlockSpec((tm, d), lambda i: (i, 0)),
        out_shape=jax.ShapeDtypeStruct((t, d), F32),
        compiler_params=_params("parallel"),
        name="out_proj",
    )(tok, proj, mem_kv, mem_kv, x, gain.reshape(1, d), w_out, w_out)


def _mlp_kernel(x_ref, gi_ref, go_ref, w1_ref, w2_ref, o_ref, h_ref, *, sub_rows):
    j = pl.program_id(1)
    last = pl.num_programs(1) - 1
    n_sub = x_ref.shape[0] // sub_rows

    def step(first, final):
        hidden = []
        for r in range(n_sub):
            rows = slice(r * sub_rows, (r + 1) * sub_rows)
            if first:
                h = _rms(x_ref[rows, :], gi_ref[...]).astype(BF16)
                h_ref[rows, :] = h
            else:
                h = h_ref[rows, :]
            u = jnp.dot(h, w1_ref[...], preferred_element_type=F32)
            hidden.append(jnp.square(jnp.maximum(u, 0.0)).astype(BF16))
        for r in range(n_sub):
            rows = slice(r * sub_rows, (r + 1) * sub_rows)
            acc = jnp.dot(hidden[r], w2_ref[...], preferred_element_type=F32)
            if not first:
                acc = o_ref[rows, :] + acc
            if final:
                acc = x_ref[rows, :] + _rms(acc, go_ref[...])
            o_ref[rows, :] = acc

    pl.when(j == 0)(lambda: step(True, False))
    pl.when(jnp.logical_and(j > 0, j < last))(lambda: step(False, False))
    pl.when(j == last)(lambda: step(False, True))


def _mlp(x, gain_in, gain_out, w1, w2, layer, *, tm=1024, tf=MLP_HIDDEN_TILE, sub_rows=512):
    t, d = x.shape
    f = w1.shape[2]
    tm = min(tm, t)
    sub_rows = min(sub_rows, tm)
    assert t % tm == 0 and tm % sub_rows == 0 and f % tf == 0 and f // tf >= 2
    return pl.pallas_call(
        functools.partial(_mlp_kernel, sub_rows=sub_rows),
        grid=(t // tm, f // tf),
        in_specs=[
            pl.BlockSpec((tm, d), lambda i, j: (i, 0)),
            pl.BlockSpec((1, d), lambda i, j: (0, 0)),
            pl.BlockSpec((1, d), lambda i, j: (0, 0)),
            pl.BlockSpec((None, d, tf), lambda i, j: (layer, 0, j)),
            pl.BlockSpec((None, tf, d), lambda i, j: (layer, j, 0)),
        ],
        out_specs=pl.BlockSpec((tm, d), lambda i, j: (i, 0)),
        out_shape=jax.ShapeDtypeStruct((t, d), F32),
        scratch_shapes=[pltpu.VMEM((tm, d), BF16)],
        compiler_params=_params("parallel", "arbitrary"),
        name="mlp",
    )(x, gain_in.reshape(1, d), gain_out.reshape(1, d), w1, w2)


def _sequences_per_step(b, l):
    nb = max(1, min(b, MIXER_STEP_TOKENS // l))
    while b % nb:
        nb -= 1
    return nb


def _retention_kernel(dec_ref, q_ref, k_ref, v_ref, g_ref, o_ref, kvb_ref, sf_ref, *, n_chunks):
    c = RET_CHUNK
    unroll = min(RET_UNROLL, n_chunks)
    log_gamma = jnp.log1p(-jnp.exp2(-dec_ref[...]))
    lgf, lgb = log_gamma[0:1, :], log_gamma[1:2, :]
    lane = lax.broadcasted_iota(jnp.int32, (1, c), 1).astype(F32)
    k_decay_f_row = jnp.exp((c - 1.0 - lane) * lgf)
    k_decay_b_row = jnp.exp(lane * lgb)
    chunk_decay_f = jnp.exp(c * lgf)
    chunk_decay_b = jnp.exp(c * lgb)
    row = lax.broadcasted_iota(jnp.int32, (c, c), 0).astype(F32)
    col = lax.broadcasted_iota(jnp.int32, (c, c), 1).astype(F32)
    diff = row - col
    decay_intra = jnp.exp(jnp.where(diff >= 0, diff * lgf, -diff * lgb))
    pos = lax.broadcasted_iota(jnp.int32, (c, HEAD_DIM), 0).astype(F32)
    q_decay_f = jnp.exp((pos + 1.0) * lgf)
    q_decay_b = jnp.exp((c - pos) * lgb)
    zero_state = jnp.zeros((HEAD_DIM, HEAD_DIM), F32)

    def one_sequence(q, k, v, g, o, base):
        def forward_group(i, sf):
            chunks = [i * unroll + u for u in range(unroll)]
            offs = [pl.multiple_of(n * c, c) for n in chunks]
            lhs = []
            for off in offs:
                kr_t = k[pl.ds(off, c), :].astype(F32).T
                lhs.append(jnp.concatenate([kr_t * k_decay_f_row, kr_t * k_decay_b_row], axis=0).astype(BF16))
            kvs = [jnp.dot(lhs[u], v[pl.ds(off, c), :], preferred_element_type=F32)
                   for u, off in enumerate(offs)]
            for n, kv in zip(chunks, kvs):
                sf_ref[base + n] = sf.astype(BF16)
                sf = chunk_decay_f * sf + kv[0:HEAD_DIM, :]
                kvb_ref[base + n] = kv[HEAD_DIM:2 * HEAD_DIM, :]
            return sf

        lax.fori_loop(0, n_chunks // unroll, forward_group, zero_state)

        def backward_group(t, sb):
            i = n_chunks // unroll - 1 - t
            chunks = [i * unroll + u for u in range(unroll)]
            offs = [pl.multiple_of(n * c, c) for n in chunks]
            scores = [lax.dot_general(q[pl.ds(off, c), :], k[pl.ds(off, c), :], NT_DIMS,
                                      preferred_element_type=F32) for off in offs]
            states = [None] * unroll
            for u in reversed(range(unroll)):
                states[u] = sb.astype(BF16)
                sb = chunk_decay_b * sb + kvb_ref[base + chunks[u]]
            ys = []
            for n, off, s, sb_n in zip(chunks, offs, scores, states):
                qr = q[pl.ds(off, c), :].astype(F32)
                lhs = jnp.concatenate([s * decay_intra, qr * q_decay_f, qr * q_decay_b], axis=1).astype(BF16)
                rhs = jnp.concatenate([v[pl.ds(off, c), :], sf_ref[base + n], sb_n], axis=0)
                ys.append(jnp.dot(lhs, rhs, preferred_element_type=F32))
            for off, y in zip(offs, ys):
                y = y * lax.rsqrt(jnp.mean(y * y, axis=-1, keepdims=True) + NORM_EPS)
                o[pl.ds(off, c), :] = (g[pl.ds(off, c), :].astype(F32) * y).astype(o.dtype)
            return sb

        lax.fori_loop(0, n_chunks // unroll, backward_group, zero_state)

    for s in range(q_ref.shape[0]):
        one_sequence(q_ref.at[s], k_ref.at[s], v_ref.at[s], g_ref.at[s], o_ref.at[s], s * n_chunks)


def _rotary_tables(l):
    half = HEAD_DIM // 2
    inv_freq = ROPE_BASE ** (-jnp.arange(half, dtype=F32) / half)
    ang = jnp.arange(l, dtype=F32)[:, None] * inv_freq[None, :]
    cos, sin = jnp.cos(ang), jnp.sin(ang)
    return jnp.concatenate([cos, cos], axis=-1), jnp.concatenate([-sin, sin], axis=-1)


def _retention(proj, decay_exp, tok_heads):
    b, l, _ = proj.shape
    assert l % RET_CHUNK == 0 and (l // RET_CHUNK) % min(RET_UNROLL, l // RET_CHUNK) == 0
    dec = jnp.broadcast_to(decay_exp.astype(F32).T[:, :, None], (tok_heads, 2, HEAD_DIM))
    nb = _sequences_per_step(b, l)
    head_spec = lambda part: pl.BlockSpec((nb, l, HEAD_DIM), lambda bi, h: (bi, 0, part * tok_heads + h))
    n_chunks = l // RET_CHUNK
    return pl.pallas_call(
        functools.partial(_retention_kernel, n_chunks=n_chunks),
        grid=(b // nb, tok_heads),
        in_specs=[pl.BlockSpec((None, 2, HEAD_DIM), lambda bi, h: (h, 0, 0)),
                  head_spec(0), head_spec(1), head_spec(2), head_spec(3)],
        out_specs=pl.BlockSpec((nb, l, HEAD_DIM), lambda bi, h: (bi, 0, h)),
        out_shape=jax.ShapeDtypeStruct((b, l, tok_heads * HEAD_DIM), BF16),
        scratch_shapes=[pltpu.VMEM((nb * n_chunks, HEAD_DIM, HEAD_DIM), F32),
                        pltpu.VMEM((nb * n_chunks, HEAD_DIM, HEAD_DIM), BF16)],
        compiler_params=_params("parallel", "parallel"),
        name="retention",
    )(dec, proj, proj, proj, proj)


def _na_kernel(bias_ref, q_ref, k_ref, v_ref, o_ref, *, rows):
    win = NA_KH * GRID_W
    unroll = min(NA_UNROLL, rows)

    def one_sequence(q, k, v, o):
        def rows_step(i, carry):
            qrows = [i * unroll + u for u in range(unroll)]
            starts = [jnp.clip(r - NA_KH // 2, 0, rows - NA_KH) for r in qrows]
            q_offs = [pl.multiple_of(r * GRID_W, GRID_W) for r in qrows]
            k_offs = [pl.multiple_of(rs * GRID_W, GRID_W) for rs in starts]
            scores = [lax.dot_general(q[pl.ds(q_off, GRID_W), :], k[pl.ds(k_off, win), :], NT_DIMS,
                                      preferred_element_type=F32)
                      for q_off, k_off in zip(q_offs, k_offs)]
            probs, denoms = [], []
            for r, rs, s in zip(qrows, starts, scores):
                s = s + bias_ref[rs - r + NA_KH - 1]
                p = jnp.exp2(s - jnp.max(s, axis=-1, keepdims=True))
                denoms.append(jnp.sum(p, axis=-1, keepdims=True))
                probs.append(p.astype(BF16))
            outs = [jnp.dot(p, v[pl.ds(k_off, win), :], preferred_element_type=F32)
                    for p, k_off in zip(probs, k_offs)]
            for q_off, out, denom in zip(q_offs, outs, denoms):
                o[pl.ds(q_off, GRID_W), :] = (out / denom).astype(o.dtype)
            return carry

        lax.fori_loop(0, rows // unroll, rows_step, 0)

    for s in range(q_ref.shape[0]):
        one_sequence(q_ref.at[s], k_ref.at[s], v_ref.at[s], o_ref.at[s])


def _na_bias_table(rpb):
    c = np.arange(GRID_W)
    win_start = np.clip(c - NA_KW // 2, 0, GRID_W - NA_KW)
    valid = (c[None, :] >= win_start[:, None]) & (c[None, :] < win_start[:, None] + NA_KW)
    dc_idx = np.clip(c[None, :] - c[:, None] + NA_KW - 1, 0, 2 * NA_KW - 2)
    toeplitz = jnp.where(valid[None, None], rpb.astype(F32)[:, :, dc_idx] * LOG2_E, -jnp.inf)
    table = jnp.stack([toeplitz[:, o:o + NA_KH] for o in range(NA_KH)], axis=1)
    table = table.transpose(0, 1, 3, 2, 4)
    return table.reshape(rpb.shape[0], NA_KH, GRID_W, NA_KH * GRID_W)


def _neighbourhood(proj, rpb, tok_heads):
    b, l, _ = proj.shape
    rows = l // GRID_W
    assert l % GRID_W == 0 and rows >= NA_KH and rows % min(NA_UNROLL, rows) == 0
    bias = _na_bias_table(rpb)
    nb = _sequences_per_step(b, l)
    head_spec = lambda part: pl.BlockSpec((nb, l, HEAD_DIM), lambda bi, h: (bi, 0, part * tok_heads + h))
    return pl.pallas_call(
        functools.partial(_na_kernel, rows=rows),
        grid=(b // nb, tok_heads),
        in_specs=[pl.BlockSpec((None, NA_KH, GRID_W, NA_KH * GRID_W), lambda bi, h: (h, 0, 0, 0)),
                  head_spec(0), head_spec(1), head_spec(2)],
        out_specs=pl.BlockSpec((nb, l, HEAD_DIM), lambda bi, h: (bi, 0, h)),
        out_shape=jax.ShapeDtypeStruct((b, l, tok_heads * HEAD_DIM), BF16),
        compiler_params=_params("parallel", "parallel"),
        name="neighbourhood",
    )(bias, proj, proj, proj)


def _trunk(x, mem, norm_gain, mem_norm_gain, w_mem_kv, w_out, w_mlp_in, w_mlp_out, w_in_ret, ret_decay,
           w_in_na, na_rpb):
    b, l, d = x.shape
    n_mem = mem.shape[1]
    tok_width = w_out.shape[1] - MEM_WIDTH
    tok_heads = tok_width // HEAD_DIM
    xf = x.reshape(b * l, d)
    memf = mem.reshape(b * n_mem, d)
    for i in range(norm_gain.shape[0]):
        g = norm_gain[i]
        mem_kv = _norm_matmul(memf, mem_norm_gain[i], w_mem_kv, i).reshape(b, n_mem, 2 * MEM_WIDTH)
        j = i // N_MIXERS
        if i % N_MIXERS == 0:
            proj = _norm_matmul(xf, g[0], w_in_ret, j, columns="retention", tok_width=tok_width,
                                seq=l).reshape(b, l, -1)
            tok = _retention(proj, ret_decay[j], tok_heads)
        else:
            proj = _norm_matmul(xf, g[0], w_in_na, j, columns="neighbourhood", tok_width=tok_width,
                                seq=l).reshape(b, l, -1)
            tok = _neighbourhood(proj, na_rpb[j], tok_heads)
        x1 = _out_proj(tok.reshape(b * l, tok_width), proj.reshape(b * l, -1), mem_kv, xf, g[1], w_out, i)
        xf = _mlp(x1, g[2], g[3], w_mlp_in, w_mlp_out, i)
    return xf.reshape(b, l, d)


def kernel(x_prompt, x_sample, mem_prompt, mem_sample, norm_gain, mem_norm_gain, w_mem_kv, w_out, w_mlp_in,
           w_mlp_out, w_in_ret, ret_decay, w_in_na, na_rpb):
    weights = [w.astype(BF16) for w in (w_mem_kv, w_out, w_mlp_in, w_mlp_out, w_in_ret)]
    w_mem_kv, w_out, w_mlp_in, w_mlp_out, w_in_ret = weights
    w_in_na = w_in_na.astype(BF16)
    run = functools.partial(_trunk, norm_gain=norm_gain, mem_norm_gain=mem_norm_gain, w_mem_kv=w_mem_kv,
                            w_out=w_out, w_mlp_in=w_mlp_in, w_mlp_out=w_mlp_out, w_in_ret=w_in_ret,
                            ret_decay=ret_decay, w_in_na=w_in_na, na_rpb=na_rpb)
    return (run(x_prompt, mem_prompt), run(x_sample, mem_sample))
```

```python
import functools

import jax
import jax.numpy as jnp
import numpy as np
from jax import lax
from jax.experimental import pallas as pl
from jax.experimental.pallas import tpu as pltpu

HEAD_DIM = 128
MEM_HEADS = 4
MEM_WIDTH = MEM_HEADS * HEAD_DIM
N_MIXERS = 2
RET_CHUNK = 128
ROPE_BASE = 10000.0
GRID_W = 64
NA_KH = 8
NA_KW = 16
NORM_EPS = 1e-6
ATTN_SCALE = HEAD_DIM ** -0.5
LOG2_E = 1.4426950408889634
MLP_HIDDEN_TILE = 1024
MIXER_STEP_TOKENS = 8192
RET_UNROLL = 64
NA_UNROLL = 64

V7X_VMEM_BYTES = 64 * 1024 * 1024
VMEM_LIMIT_BYTES = V7X_VMEM_BYTES - 4 * 1024 * 1024

F32 = jnp.float32
BF16 = jnp.bfloat16
NT_DIMS = (((1,), (1,)), ((), ()))


def _params(*semantics):
    return pltpu.CompilerParams(dimension_semantics=semantics, vmem_limit_bytes=VMEM_LIMIT_BYTES)


def _rms(x, gain):
    return x * lax.rsqrt(jnp.mean(x * x, axis=-1, keepdims=True) + NORM_EPS) * gain


def _rotate(x, cos2, sin2):
    return x * cos2 + pltpu.roll(x, HEAD_DIM // 2, axis=1) * sin2


def _retention_columns(res, cos2, sin2, tok_width):
    heads = tok_width // HEAD_DIM
    blocks = []
    for c in range(res.shape[1] // HEAD_DIM):
        blk = res[:, c * HEAD_DIM:(c + 1) * HEAD_DIM]
        part = c // heads
        if part == 0:
            blk = _rotate(blk, cos2, sin2)
        elif part == 1:
            blk = _rotate(blk, cos2, sin2) * ATTN_SCALE
        elif part == 3:
            blk = blk * (1.0 / (1.0 + jnp.exp(-blk)))
        blocks.append(blk.astype(BF16))
    return jnp.concatenate(blocks, axis=1)


def _neighbourhood_columns(res, tok_width):
    q = res[:, :tok_width] * (ATTN_SCALE * LOG2_E)
    return jnp.concatenate([q.astype(BF16), res[:, tok_width:].astype(BF16)], axis=1)


def _norm_matmul_kernel(x_ref, g_ref, w_ref, *rest, sub_rows, columns, tok_width):
    o_ref = rest[-1]
    gain = g_ref[...]
    for r in range(x_ref.shape[0] // sub_rows):
        rows = slice(r * sub_rows, (r + 1) * sub_rows)
        h = _rms(x_ref[rows, :], gain).astype(BF16)
        res = jnp.dot(h, w_ref[...], preferred_element_type=F32)
        if columns == "retention":
            cos_ref, sin_ref = rest[0], rest[1]
            res = _retention_columns(res, cos_ref[rows, :], sin_ref[rows, :], tok_width)
        elif columns == "neighbourhood":
            res = _neighbourhood_columns(res, tok_width)
        o_ref[rows, :] = res.astype(o_ref.dtype)


def _norm_matmul(x, gain, w, layer, *, columns=None, tok_width=None, seq=None, tm=512, sub_rows=256):
    t, d = x.shape
    n = w.shape[2]
    tm = min(tm, t if seq is None else seq)
    sub_rows = min(sub_rows, tm)
    assert t % tm == 0 and tm % sub_rows == 0
    operands = [x, gain.reshape(1, d), w]
    in_specs = [
        pl.BlockSpec((tm, d), lambda i: (i, 0)),
        pl.BlockSpec((1, d), lambda i: (0, 0)),
        pl.BlockSpec((None, d, n), lambda i: (layer, 0, 0), pipeline_mode=pl.Buffered(1)),
    ]
    if columns == "retention":
        assert seq % tm == 0
        tiles_per_seq = seq // tm
        operands += list(_rotary_tables(seq))
        in_specs += [pl.BlockSpec((tm, HEAD_DIM), lambda i: (i % tiles_per_seq, 0)),
                     pl.BlockSpec((tm, HEAD_DIM), lambda i: (i % tiles_per_seq, 0))]
    return pl.pallas_call(
        functools.partial(_norm_matmul_kernel, sub_rows=sub_rows, columns=columns, tok_width=tok_width),
        grid=(t // tm,),
        in_specs=in_specs,
        out_specs=pl.BlockSpec((tm, n), lambda i: (i, 0)),
        out_shape=jax.ShapeDtypeStruct((t, n), BF16),
        compiler_params=_params("parallel"),
        name="norm_matmul",
    )(*operands)


def _mem_scores(q, k_ref):
    return [lax.dot_general(q[:, h * HEAD_DIM:(h + 1) * HEAD_DIM], k_ref[:, h * HEAD_DIM:(h + 1) * HEAD_DIM],
                            NT_DIMS, preferred_element_type=F32) for h in range(MEM_HEADS)]


def _mem_values(scores, v_ref):
    probs, denoms = [], []
    for s in scores:
        s = s * ATTN_SCALE
        p = jnp.exp(s - jnp.max(s, axis=-1, keepdims=True))
        denoms.append(jnp.sum(p, axis=-1, keepdims=True))
        probs.append(p.astype(BF16))
    outs = [jnp.dot(p, v_ref[:, h * HEAD_DIM:(h + 1) * HEAD_DIM], preferred_element_type=F32)
            for h, p in enumerate(probs)]
    return jnp.concatenate([(o / dn).astype(BF16) for o, dn in zip(outs, denoms)], axis=1)


def _out_proj_kernel(tok_ref, qm_ref, k_ref, v_ref, x_ref, g_ref, wt_ref, wm_ref, o_ref, *, sub_rows):
    gain = g_ref[...]
    n_sub = x_ref.shape[0] // sub_rows
    rows = [slice(r * sub_rows, (r + 1) * sub_rows) for r in range(n_sub)]
    mem = _mem_values(_mem_scores(qm_ref[rows[0], :], k_ref), v_ref)
    for r in range(n_sub):
        if r + 1 < n_sub:
            next_scores = _mem_scores(qm_ref[rows[r + 1], :], k_ref)
        mixed = (jnp.dot(tok_ref[rows[r], :], wt_ref[...], preferred_element_type=F32)
                 + jnp.dot(mem, wm_ref[...], preferred_element_type=F32))
        if r + 1 < n_sub:
            mem = _mem_values(next_scores, v_ref)
        o_ref[rows[r], :] = x_ref[rows[r], :] + _rms(mixed, gain)


def _out_proj(tok, proj, mem_kv, x, gain, w_out, layer, *, tm=1024, sub_rows=256):
    t, d = x.shape
    b, m, _ = mem_kv.shape
    seq = t // b
    tm = min(tm, seq)
    sub_rows = min(sub_rows, tm)
    tw, n = tok.shape[1], proj.shape[1]
    assert seq % tm == 0 and tm % sub_rows == 0 and tw % MEM_WIDTH == 0 and n % MEM_WIDTH == 0
    assert w_out.shape[1] == tw + MEM_WIDTH
    tiles_per_seq = seq // tm
    q_block = n // MEM_WIDTH - 1
    return pl.pallas_call(
        functools.partial(_out_proj_kernel, sub_rows=sub_rows),
        grid=(t // tm,),
        in_specs=[
            pl.BlockSpec((tm, tw), lambda i: (i, 0)),
            pl.BlockSpec((tm, MEM_WIDTH), lambda i: (i, q_block)),
            pl.BlockSpec((None, m, MEM_WIDTH), lambda i: (i // tiles_per_seq, 0, 0)),
            pl.BlockSpec((None, m, MEM_WIDTH), lambda i: (i // tiles_per_seq, 0, 1)),
            pl.BlockSpec((tm, d), lambda i: (i, 0)),
            pl.BlockSpec((1, d), lambda i: (0, 0)),
            pl.BlockSpec((None, tw, d), lambda i: (layer, 0, 0), pipeline_mode=pl.Buffered(1)),
            pl.BlockSpec((None, MEM_WIDTH, d), lambda i: (layer, tw // MEM_WIDTH, 0),
                         pipeline_mode=pl.Buffered(1)),
        ],
        out_specs=pl.BlockSpec((tm, d), lambda i: (i, 0)),
        out_shape=jax.ShapeDtypeStruct((t, d), F32),
        compiler_params=_params("parallel"),
        name="out_proj",
    )(tok, proj, mem_kv, mem_kv, x, gain.reshape(1, d), w_out, w_out)


def _mlp_kernel(x_ref, gi_ref, go_ref, w1_ref, w2_ref, o_ref, h_ref, *, sub_rows):
    j = pl.program_id(1)
    last = pl.num_programs(1) - 1
    n_sub = x_ref.shape[0] // sub_rows

    def step(first, final):
        hidden = []
        for r in range(n_sub):
            rows = slice(r * sub_rows, (r + 1) * sub_rows)
            if first:
                h = _rms(x_ref[rows, :], gi_ref[...]).astype(BF16)
                h_ref[rows, :] = h
            else:
                h = h_ref[rows, :]
            u = jnp.dot(h, w1_ref[...], preferred_element_type=F32)
            hidden.append(jnp.square(jnp.maximum(u, 0.0)).astype(BF16))
        for r in range(n_sub):
            rows = slice(r * sub_rows, (r + 1) * sub_rows)
            acc = jnp.dot(hidden[r], w2_ref[...], preferred_element_type=F32)
            if not first:
                acc = o_ref[rows, :] + acc
            if final:
                acc = x_ref[rows, :] + _rms(acc, go_ref[...])
            o_ref[rows, :] = acc

    pl.when(j == 0)(lambda: step(True, False))
    pl.when(jnp.logical_and(j > 0, j < last))(lambda: step(False, False))
    pl.when(j == last)(lambda: step(False, True))


def _mlp(x, gain_in, gain_out, w1, w2, layer, *, tm=1024, tf=MLP_HIDDEN_TILE, sub_rows=512):
    t, d = x.shape
    f = w1.shape[2]
    tm = min(tm, t)
    sub_rows = min(sub_rows, tm)
    assert t % tm == 0 and tm % sub_rows == 0 and f % tf == 0 and f // tf >= 2
    return pl.pallas_call(
        functools.partial(_mlp_kernel, sub_rows=sub_rows),
        grid=(t // tm, f // tf),
        in_specs=[
            pl.BlockSpec((tm, d), lambda i, j: (i, 0)),
            pl.BlockSpec((1, d), lambda i, j: (0, 0)),
            pl.BlockSpec((1, d), lambda i, j: (0, 0)),
            pl.BlockSpec((None, d, tf), lambda i, j: (layer, 0, j)),
            pl.BlockSpec((None, tf, d), lambda i, j: (layer, j, 0)),
        ],
        out_specs=pl.BlockSpec((tm, d), lambda i, j: (i, 0)),
        out_shape=jax.ShapeDtypeStruct((t, d), F32),
        scratch_shapes=[pltpu.VMEM((tm, d), BF16)],
        compiler_params=_params("parallel", "arbitrary"),
        name="mlp",
    )(x, gain_in.reshape(1, d), gain_out.reshape(1, d), w1, w2)


def _sequences_per_step(b, l):
    nb = max(1, min(b, MIXER_STEP_TOKENS // l))
    while b % nb:
        nb -= 1
    return nb


def _retention_kernel(dec_ref, q_ref, k_ref, v_ref, g_ref, o_ref, kvb_ref, sf_ref, *, n_chunks):
    c = RET_CHUNK
    unroll = min(RET_UNROLL, n_chunks)
    log_gamma = jnp.log1p(-jnp.exp2(-dec_ref[...]))
    lgf, lgb = log_gamma[0:1, :], log_gamma[1:2, :]
    lane = lax.broadcasted_iota(jnp.int32, (1, c), 1).astype(F32)
    k_decay_f_row = jnp.exp((c - 1.0 - lane) * lgf)
    k_decay_b_row = jnp.exp(lane * lgb)
    chunk_decay_f = jnp.exp(c * lgf)
    chunk_decay_b = jnp.exp(c * lgb)
    row = lax.broadcasted_iota(jnp.int32, (c, c), 0).astype(F32)
    col = lax.broadcasted_iota(jnp.int32, (c, c), 1).astype(F32)
    diff = row - col
    decay_intra = jnp.exp(jnp.where(diff >= 0, diff * lgf, -diff * lgb))
    pos = lax.broadcasted_iota(jnp.int32, (c, HEAD_DIM), 0).astype(F32)
    q_decay_f = jnp.exp((pos + 1.0) * lgf)
    q_decay_b = jnp.exp((c - pos) * lgb)
    zero_state = jnp.zeros((HEAD_DIM, HEAD_DIM), F32)

    def one_sequence(q, k, v, g, o, base):
        def forward_group(i, sf):
            chunks = [i * unroll + u for u in range(unroll)]
            offs = [pl.multiple_of(n * c, c) for n in chunks]
            lhs = []
            for off in offs:
                kr_t = k[pl.ds(off, c), :].astype(F32).T
                lhs.append(jnp.concatenate([kr_t * k_decay_f_row, kr_t * k_decay_b_row], axis=0).astype(BF16))
            kvs = [jnp.dot(lhs[u], v[pl.ds(off, c), :], preferred_element_type=F32)
                   for u, off in enumerate(offs)]
            for n, kv in zip(chunks, kvs):
                sf_ref[base + n] = sf.astype(BF16)
                sf = chunk_decay_f * sf + kv[0:HEAD_DIM, :]
                kvb_ref[base + n] = kv[HEAD_DIM:2 * HEAD_DIM, :]
            return sf

        lax.fori_loop(0, n_chunks // unroll, forward_group, zero_state)

        def backward_group(t, sb):
            i = n_chunks // unroll - 1 - t
            chunks = [i * unroll + u for u in range(unroll)]
            offs = [pl.multiple_of(n * c, c) for n in chunks]
            scores = [lax.dot_general(q[pl.ds(off, c), :], k[pl.ds(off, c), :], NT_DIMS,
                                      preferred_element_type=F32) for off in offs]
            states = [None] * unroll
            for u in reversed(range(unroll)):
                states[u] = sb.astype(BF16)
                sb = chunk_decay_b * sb + kvb_ref[base + chunks[u]]
            ys = []
            for n, off, s, sb_n in zip(chunks, offs, scores, states):
                qr = q[pl.ds(off, c), :].astype(F32)
                lhs = jnp.concatenate([s * decay_intra, qr * q_decay_f, qr * q_decay_b], axis=1).astype(BF16)
                rhs = jnp.concatenate([v[pl.ds(off, c), :], sf_ref[base + n], sb_n], axis=0)
                ys.append(jnp.dot(lhs, rhs, preferred_element_type=F32))
            for off, y in zip(offs, ys):
                y = y * lax.rsqrt(jnp.mean(y * y, axis=-1, keepdims=True) + NORM_EPS)
                o[pl.ds(off, c), :] = (g[pl.ds(off, c), :].astype(F32) * y).astype(o.dtype)
            return sb

        lax.fori_loop(0, n_chunks // unroll, backward_group, zero_state)

    for s in range(q_ref.shape[0]):
        one_sequence(q_ref.at[s], k_ref.at[s], v_ref.at[s], g_ref.at[s], o_ref.at[s], s * n_chunks)


def _rotary_tables(l):
    half = HEAD_DIM // 2
    inv_freq = ROPE_BASE ** (-jnp.arange(half, dtype=F32) / half)
    ang = jnp.arange(l, dtype=F32)[:, None] * inv_freq[None, :]
    cos, sin = jnp.cos(ang), jnp.sin(ang)
    return jnp.concatenate([cos, cos], axis=-1), jnp.concatenate([-sin, sin], axis=-1)


def _retention(proj, decay_exp, tok_heads):
    b, l, _ = proj.shape
    assert l % RET_CHUNK == 0 and (l // RET_CHUNK) % min(RET_UNROLL, l // RET_CHUNK) == 0
    dec = jnp.broadcast_to(decay_exp.astype(F32).T[:, :, None], (tok_heads, 2, HEAD_DIM))
    nb = _sequences_per_step(b, l)
    head_spec = lambda part: pl.BlockSpec((nb, l, HEAD_DIM), lambda bi, h: (bi, 0, part * tok_heads + h))
    n_chunks = l // RET_CHUNK
    return pl.pallas_call(
        functools.partial(_retention_kernel, n_chunks=n_chunks),
        grid=(b // nb, tok_heads),
        in_specs=[pl.BlockSpec((None, 2, HEAD_DIM), lambda bi, h: (h, 0, 0)),
                  head_spec(0), head_spec(1), head_spec(2), head_spec(3)],
        out_specs=pl.BlockSpec((nb, l, HEAD_DIM), lambda bi, h: (bi, 0, h)),
        out_shape=jax.ShapeDtypeStruct((b, l, tok_heads * HEAD_DIM), BF16),
        scratch_shapes=[pltpu.VMEM((nb * n_chunks, HEAD_DIM, HEAD_DIM), F32),
                        pltpu.VMEM((nb * n_chunks, HEAD_DIM, HEAD_DIM), BF16)],
        compiler_params=_params("parallel", "parallel"),
        name="retention",
    )(dec, proj, proj, proj, proj)


def _na_kernel(bias_ref, q_ref, k_ref, v_ref, o_ref, *, rows):
    win = NA_KH * GRID_W
    unroll = min(NA_UNROLL, rows)

    def one_sequence(q, k, v, o):
        def rows_step(i, carry):
            qrows = [i * unroll + u for u in range(unroll)]
            starts = [jnp.clip(r - NA_KH // 2, 0, rows - NA_KH) for r in qrows]
            q_offs = [pl.multiple_of(r * GRID_W, GRID_W) for r in qrows]
            k_offs = [pl.multiple_of(rs * GRID_W, GRID_W) for rs in starts]
            scores = [lax.dot_general(q[pl.ds(q_off, GRID_W), :], k[pl.ds(k_off, win), :], NT_DIMS,
                                      preferred_element_type=F32)
                      for q_off, k_off in zip(q_offs, k_offs)]
            probs, denoms = [], []
            for r, rs, s in zip(qrows, starts, scores):
                first = rs - r + NA_KH - 1
                s = s + jnp.concatenate([bias_ref[first + 2 * j] for j in range(NA_KH // 2)], axis=1)
                p = jnp.exp2(s - jnp.max(s, axis=-1, keepdims=True))
                denoms.append(jnp.sum(p, axis=-1, keepdims=True))
                probs.append(p.astype(BF16))
            outs = [jnp.dot(p, v[pl.ds(k_off, win), :], preferred_element_type=F32)
                    for p, k_off in zip(probs, k_offs)]
            for q_off, out, denom in zip(q_offs, outs, denoms):
                o[pl.ds(q_off, GRID_W), :] = (out / denom).astype(o.dtype)
            return carry

        lax.fori_loop(0, rows // unroll, rows_step, 0)

    for s in range(q_ref.shape[0]):
        one_sequence(q_ref.at[s], k_ref.at[s], v_ref.at[s], o_ref.at[s])


def _na_bias_table(rpb):
    c = np.arange(GRID_W)
    win_start = np.clip(c - NA_KW // 2, 0, GRID_W - NA_KW)
    valid = (c[None, :] >= win_start[:, None]) & (c[None, :] < win_start[:, None] + NA_KW)
    dc_idx = np.clip(c[None, :] - c[:, None] + NA_KW - 1, 0, 2 * NA_KW - 2)
    toeplitz = jnp.where(valid[None, None], rpb.astype(F32)[:, :, dc_idx] * LOG2_E, -jnp.inf)
    return jnp.concatenate([toeplitz[:, :-1], toeplitz[:, 1:]], axis=-1)


def _neighbourhood(proj, rpb, tok_heads):
    b, l, _ = proj.shape
    rows = l // GRID_W
    assert l % GRID_W == 0 and rows >= NA_KH and rows % min(NA_UNROLL, rows) == 0
    bias = _na_bias_table(rpb)
    nb = _sequences_per_step(b, l)
    head_spec = lambda part: pl.BlockSpec((nb, l, HEAD_DIM), lambda bi, h: (bi, 0, part * tok_heads + h))
    return pl.pallas_call(
        functools.partial(_na_kernel, rows=rows),
        grid=(b // nb, tok_heads),
        in_specs=[pl.BlockSpec((None, 2 * NA_KH - 2, GRID_W, 2 * GRID_W), lambda bi, h: (h, 0, 0, 0)),
                  head_spec(0), head_spec(1), head_spec(2)],
        out_specs=pl.BlockSpec((nb, l, HEAD_DIM), lambda bi, h: (bi, 0, h)),
        out_shape=jax.ShapeDtypeStruct((b, l, tok_heads * HEAD_DIM), BF16),
        compiler_params=_params("parallel", "parallel"),
        name="neighbourhood",
    )(bias, proj, proj, proj)


def _trunk(x, mem, norm_gain, mem_norm_gain, w_mem_kv, w_out, w_mlp_in, w_mlp_out, w_in_ret, ret_decay,
           w_in_na, na_rpb):
    b, l, d = x.shape
    n_mem = mem.shape[1]
    tok_width = w_out.shape[1] - MEM_WIDTH
    tok_heads = tok_width // HEAD_DIM
    xf = x.reshape(b * l, d)
    memf = mem.reshape(b * n_mem, d)
    for i in range(norm_gain.shape[0]):
        g = norm_gain[i]
        mem_kv = _norm_matmul(memf, mem_norm_gain[i], w_mem_kv, i).reshape(b, n_mem, 2 * MEM_WIDTH)
        j = i // N_MIXERS
        if i % N_MIXERS == 0:
            proj = _norm_matmul(xf, g[0], w_in_ret, j, columns="retention", tok_width=tok_width,
                                seq=l).reshape(b, l, -1)
            tok = _retention(proj, ret_decay[j], tok_heads)
        else:
            proj = _norm_matmul(xf, g[0], w_in_na, j, columns="neighbourhood", tok_width=tok_width,
                                seq=l).reshape(b, l, -1)
            tok = _neighbourhood(proj, na_rpb[j], tok_heads)
        x1 = _out_proj(tok.reshape(b * l, tok_width), proj.reshape(b * l, -1), mem_kv, xf, g[1], w_out, i)
        xf = _mlp(x1, g[2], g[3], w_mlp_in, w_mlp_out, i)
    return xf.reshape(b, l, d)


def kernel(x_prompt, x_sample, mem_prompt, mem_sample, norm_gain, mem_norm_gain, w_mem_kv, w_out, w_mlp_in,
           w_mlp_out, w_in_ret, ret_decay, w_in_na, na_rpb):
    weights = [w.astype(BF16) for w in (w_mem_kv, w_out, w_mlp_in, w_mlp_out, w_in_ret)]
    w_mem_kv, w_out, w_mlp_in, w_mlp_out, w_in_ret = weights
    w_in_na = w_in_na.astype(BF16)
    run = functools.partial(_trunk, norm_gain=norm_gain, mem_norm_gain=mem_norm_gain, w_mem_kv=w_mem_kv,
                            w_out=w_out, w_mlp_in=w_mlp_in, w_mlp_out=w_mlp_out, w_in_ret=w_in_ret,
                            ret_decay=ret_decay, w_in_na=w_in_na, na_rpb=na_rpb)
    return (run(x_prompt, mem_prompt), run(x_sample, mem_sample))
```

```python
import functools

import jax
import jax.numpy as jnp
import numpy as np
from jax import lax
from jax.experimental import pallas as pl
from jax.experimental.pallas import tpu as pltpu

HEAD_DIM = 128
MEM_HEADS = 4
MEM_WIDTH = MEM_HEADS * HEAD_DIM
N_MIXERS = 2
RET_CHUNK = 128
ROPE_BASE = 10000.0
GRID_W = 64
NA_KH = 8
NA_KW = 16
NORM_EPS = 1e-6
ATTN_SCALE = HEAD_DIM ** -0.5
LOG2_E = 1.4426950408889634
MLP_HIDDEN_TILE = 1024
MIXER_STEP_TOKENS = 8192
RET_UNROLL = 64
NA_UNROLL = 64

V7X_VMEM_BYTES = 64 * 1024 * 1024
VMEM_LIMIT_BYTES = V7X_VMEM_BYTES - 4 * 1024 * 1024

F32 = jnp.float32
BF16 = jnp.bfloat16
NT_DIMS = (((1,), (1,)), ((), ()))


def _params(*semantics):
    return pltpu.CompilerParams(dimension_semantics=semantics, vmem_limit_bytes=VMEM_LIMIT_BYTES)


def _rms(x, gain):
    return x * lax.rsqrt(jnp.mean(x * x, axis=-1, keepdims=True) + NORM_EPS) * gain


def _rotate(x, cos2, sin2):
    return x * cos2 + pltpu.roll(x, HEAD_DIM // 2, axis=1) * sin2


def _retention_columns(res, cos2, sin2, tok_width):
    heads = tok_width // HEAD_DIM
    blocks = []
    for c in range(res.shape[1] // HEAD_DIM):
        blk = res[:, c * HEAD_DIM:(c + 1) * HEAD_DIM]
        part = c // heads
        if part == 0:
            blk = _rotate(blk, cos2, sin2)
        elif part == 1:
            blk = _rotate(blk, cos2, sin2) * ATTN_SCALE
        elif part == 3:
            blk = blk * (1.0 / (1.0 + jnp.exp(-blk)))
        blocks.append(blk.astype(BF16))
    return jnp.concatenate(blocks, axis=1)


def _neighbourhood_columns(res, tok_width):
    q = res[:, :tok_width] * (ATTN_SCALE * LOG2_E)
    return jnp.concatenate([q.astype(BF16), res[:, tok_width:].astype(BF16)], axis=1)


def _norm_matmul_kernel(x_ref, g_ref, w_ref, *rest, sub_rows, columns, tok_width):
    o_ref = rest[-1]
    gain = g_ref[...]
    for r in range(x_ref.shape[0] // sub_rows):
        rows = slice(r * sub_rows, (r + 1) * sub_rows)
        h = _rms(x_ref[rows, :], gain).astype(BF16)
        res = jnp.dot(h, w_ref[...], preferred_element_type=F32)
        if columns == "retention":
            cos_ref, sin_ref = rest[0], rest[1]
            res = _retention_columns(res, cos_ref[rows, :], sin_ref[rows, :], tok_width)
        elif columns == "neighbourhood":
            res = _neighbourhood_columns(res, tok_width)
        o_ref[rows, :] = res.astype(o_ref.dtype)


def _norm_matmul(x, gain, w, layer, *, columns=None, tok_width=None, seq=None, tm=512, sub_rows=256):
    t, d = x.shape
    n = w.shape[2]
    tm = min(tm, t if seq is None else seq)
    sub_rows = min(sub_rows, tm)
    assert t % tm == 0 and tm % sub_rows == 0
    operands = [x, gain.reshape(1, d), w]
    in_specs = [
        pl.BlockSpec((tm, d), lambda i: (i, 0)),
        pl.BlockSpec((1, d), lambda i: (0, 0)),
        pl.BlockSpec((None, d, n), lambda i: (layer, 0, 0), pipeline_mode=pl.Buffered(1)),
    ]
    if columns == "retention":
        assert seq % tm == 0
        tiles_per_seq = seq // tm
        operands += list(_rotary_tables(seq))
        in_specs += [pl.BlockSpec((tm, HEAD_DIM), lambda i: (i % tiles_per_seq, 0)),
                     pl.BlockSpec((tm, HEAD_DIM), lambda i: (i % tiles_per_seq, 0))]
    return pl.pallas_call(
        functools.partial(_norm_matmul_kernel, sub_rows=sub_rows, columns=columns, tok_width=tok_width),
        grid=(t // tm,),
        in_specs=in_specs,
        out_specs=pl.BlockSpec((tm, n), lambda i: (i, 0)),
        out_shape=jax.ShapeDtypeStruct((t, n), BF16),
        compiler_params=_params("parallel"),
        name="norm_matmul",
    )(*operands)


def _mem_scores(q, k_ref):
    return [lax.dot_general(q[:, h * HEAD_DIM:(h + 1) * HEAD_DIM], k_ref[:, h * HEAD_DIM:(h + 1) * HEAD_DIM],
                            NT_DIMS, preferred_element_type=F32) for h in range(MEM_HEADS)]


def _mem_values(scores, v_ref):
    probs, denoms = [], []
    for s in scores:
        s = s * ATTN_SCALE
        p = jnp.exp(s - jnp.max(s, axis=-1, keepdims=True))
        denoms.append(jnp.sum(p, axis=-1, keepdims=True))
        probs.append(p.astype(BF16))
    outs = [jnp.dot(p, v_ref[:, h * HEAD_DIM:(h + 1) * HEAD_DIM], preferred_element_type=F32)
            for h, p in enumerate(probs)]
    return jnp.concatenate([(o / dn).astype(BF16) for o, dn in zip(outs, denoms)], axis=1)


def _out_proj_kernel(tok_ref, qm_ref, k_ref, v_ref, x_ref, g_ref, wt_ref, wm_ref, o_ref, *, sub_rows):
    gain = g_ref[...]
    n_sub = x_ref.shape[0] // sub_rows
    rows = [slice(r * sub_rows, (r + 1) * sub_rows) for r in range(n_sub)]
    mem = _mem_values(_mem_scores(qm_ref[rows[0], :], k_ref), v_ref)
    for r in range(n_sub):
        if r + 1 < n_sub:
            next_scores = _mem_scores(qm_ref[rows[r + 1], :], k_ref)
        mixed = (jnp.dot(tok_ref[rows[r], :], wt_ref[...], preferred_element_type=F32)
                 + jnp.dot(mem, wm_ref[...], preferred_element_type=F32))
        if r + 1 < n_sub:
            mem = _mem_values(next_scores, v_ref)
        o_ref[rows[r], :] = x_ref[rows[r], :] + _rms(mixed, gain)


def _out_proj(tok, proj, mem_kv, x, gain, w_out, layer, *, tm=1024, sub_rows=256):
    t, d = x.shape
    b, m, _ = mem_kv.shape
    seq = t // b
    tm = min(tm, seq)
    sub_rows = min(sub_rows, tm)
    tw, n = tok.shape[1], proj.shape[1]
    assert seq % tm == 0 and tm % sub_rows == 0 and tw % MEM_WIDTH == 0 and n % MEM_WIDTH == 0
    assert w_out.shape[1] == tw + MEM_WIDTH
    tiles_per_seq = seq // tm
    q_block = n // MEM_WIDTH - 1
    return pl.pallas_call(
        functools.partial(_out_proj_kernel, sub_rows=sub_rows),
        grid=(t // tm,),
        in_specs=[
            pl.BlockSpec((tm, tw), lambda i: (i, 0)),
            pl.BlockSpec((tm, MEM_WIDTH), lambda i: (i, q_block)),
            pl.BlockSpec((None, m, MEM_WIDTH), lambda i: (i // tiles_per_seq, 0, 0)),
            pl.BlockSpec((None, m, MEM_WIDTH), lambda i: (i // tiles_per_seq, 0, 1)),
            pl.BlockSpec((tm, d), lambda i: (i, 0)),
            pl.BlockSpec((1, d), lambda i: (0, 0)),
            pl.BlockSpec((None, tw, d), lambda i: (layer, 0, 0), pipeline_mode=pl.Buffered(1)),
            pl.BlockSpec((None, MEM_WIDTH, d), lambda i: (layer, tw // MEM_WIDTH, 0),
                         pipeline_mode=pl.Buffered(1)),
        ],
        out_specs=pl.BlockSpec((tm, d), lambda i: (i, 0)),
        out_shape=jax.ShapeDtypeStruct((t, d), F32),
        compiler_params=_params("parallel"),
        name="out_proj",
    )(tok, proj, mem_kv, mem_kv, x, gain.reshape(1, d), w_out, w_out)


def _mlp_kernel(x_ref, gi_ref, go_ref, w1_ref, w2_ref, o_ref, h_ref, *, sub_rows):
    j = pl.program_id(1)
    last = pl.num_programs(1) - 1
    n_sub = x_ref.shape[0] // sub_rows

    def step(first, final):
        hidden = []
        for r in range(n_sub):
            rows = slice(r * sub_rows, (r + 1) * sub_rows)
            if first:
                h = _rms(x_ref[rows, :], gi_ref[...]).astype(BF16)
                h_ref[rows, :] = h
            else:
                h = h_ref[rows, :]
            u = jnp.dot(h, w1_ref[...], preferred_element_type=F32)
            hidden.append(jnp.square(jnp.maximum(u, 0.0)).astype(BF16))
        for r in range(n_sub):
            rows = slice(r * sub_rows, (r + 1) * sub_rows)
            acc = jnp.dot(hidden[r], w2_ref[...], preferred_element_type=F32)
            if not first:
                acc = o_ref[rows, :] + acc
            if final:
                acc = x_ref[rows, :] + _rms(acc, go_ref[...])
            o_ref[rows, :] = acc

    pl.when(j == 0)(lambda: step(True, False))
    pl.when(jnp.logical_and(j > 0, j < last))(lambda: step(False, False))
    pl.when(j == last)(lambda: step(False, True))


def _mlp(x, gain_in, gain_out, w1, w2, layer, *, tm=1024, tf=MLP_HIDDEN_TILE, sub_rows=512):
    t, d = x.shape
    f = w1.shape[2]
    tm = min(tm, t)
    sub_rows = min(sub_rows, tm)
    assert t % tm == 0 and tm % sub_rows == 0 and f % tf == 0 and f // tf >= 2
    return pl.pallas_call(
        functools.partial(_mlp_kernel, sub_rows=sub_rows),
        grid=(t // tm, f // tf),
        in_specs=[
            pl.BlockSpec((tm, d), lambda i, j: (i, 0)),
            pl.BlockSpec((1, d), lambda i, j: (0, 0)),
            pl.BlockSpec((1, d), lambda i, j: (0, 0)),
            pl.BlockSpec((None, d, tf), lambda i, j: (layer, 0, j)),
            pl.BlockSpec((None, tf, d), lambda i, j: (layer, j, 0)),
        ],
        out_specs=pl.BlockSpec((tm, d), lambda i, j: (i, 0)),
        out_shape=jax.ShapeDtypeStruct((t, d), F32),
        scratch_shapes=[pltpu.VMEM((tm, d), BF16)],
        compiler_params=_params("parallel", "arbitrary"),
        name="mlp",
    )(x, gain_in.reshape(1, d), gain_out.reshape(1, d), w1, w2)


def _sequences_per_step(b, l):
    nb = max(1, min(b, MIXER_STEP_TOKENS // l))
    while b % nb:
        nb -= 1
    return nb


def _retention_kernel(dec_ref, q_ref, k_ref, v_ref, g_ref, o_ref, kvb_ref, sf_ref, *, n_chunks):
    c = RET_CHUNK
    unroll = min(RET_UNROLL, n_chunks)
    log_gamma = jnp.log1p(-jnp.exp2(-dec_ref[...]))
    lgf, lgb = log_gamma[0:1, :], log_gamma[1:2, :]
    lane = lax.broadcasted_iota(jnp.int32, (1, c), 1).astype(F32)
    k_decay_f_row = jnp.exp((c - 1.0 - lane) * lgf)
    k_decay_b_row = jnp.exp(lane * lgb)
    chunk_decay_f = jnp.exp(c * lgf)
    chunk_decay_b = jnp.exp(c * lgb)
    row = lax.broadcasted_iota(jnp.int32, (c, c), 0).astype(F32)
    col = lax.broadcasted_iota(jnp.int32, (c, c), 1).astype(F32)
    diff = row - col
    decay_intra = jnp.exp(jnp.where(diff >= 0, diff * lgf, -diff * lgb))
    pos = lax.broadcasted_iota(jnp.int32, (c, HEAD_DIM), 0).astype(F32)
    q_decay_f = jnp.exp((pos + 1.0) * lgf)
    q_decay_b = jnp.exp((c - pos) * lgb)
    zero_state = jnp.zeros((HEAD_DIM, HEAD_DIM), F32)

    def one_sequence(q, k, v, g, o, base):
        def forward_group(i, sf):
            chunks = [i * unroll + u for u in range(unroll)]
            offs = [pl.multiple_of(n * c, c) for n in chunks]
            lhs = []
            for off in offs:
                kr_t = k[pl.ds(off, c), :].astype(F32).T
                lhs.append(jnp.concatenate([kr_t * k_decay_f_row, kr_t * k_decay_b_row], axis=0).astype(BF16))
            kvs = [jnp.dot(lhs[u], v[pl.ds(off, c), :], preferred_element_type=F32)
                   for u, off in enumerate(offs)]
            for n, kv in zip(chunks, kvs):
                sf_ref[base + n] = sf.astype(BF16)
                sf = chunk_decay_f * sf + kv[0:HEAD_DIM, :]
                kvb_ref[base + n] = kv[HEAD_DIM:2 * HEAD_DIM, :]
            return sf

        lax.fori_loop(0, n_chunks // unroll, forward_group, zero_state)

        def backward_group(t, sb):
            i = n_chunks // unroll - 1 - t
            chunks = [i * unroll + u for u in range(unroll)]
            offs = [pl.multiple_of(n * c, c) for n in chunks]
            scores = [lax.dot_general(q[pl.ds(off, c), :], k[pl.ds(off, c), :], NT_DIMS,
                                      preferred_element_type=F32) for off in offs]
            states = [None] * unroll
            for u in reversed(range(unroll)):
                states[u] = sb.astype(BF16)
                sb = chunk_decay_b * sb + kvb_ref[base + chunks[u]]
            ys = []
            for n, off, s, sb_n in zip(chunks, offs, scores, states):
                qr = q[pl.ds(off, c), :].astype(F32)
                lhs = jnp.concatenate([s * decay_intra, qr * q_decay_f, qr * q_decay_b], axis=1).astype(BF16)
                rhs = jnp.concatenate([v[pl.ds(off, c), :], sf_ref[base + n], sb_n], axis=0)
                ys.append(jnp.dot(lhs, rhs, preferred_element_type=F32))
            for off, y in zip(offs, ys):
                y = y * lax.rsqrt(jnp.mean(y * y, axis=-1, keepdims=True) + NORM_EPS)
                o[pl.ds(off, c), :] = (g[pl.ds(off, c), :].astype(F32) * y).astype(o.dtype)
            return sb

        lax.fori_loop(0, n_chunks // unroll, backward_group, zero_state)

    for s in range(q_ref.shape[0]):
        one_sequence(q_ref.at[s], k_ref.at[s], v_ref.at[s], g_ref.at[s], o_ref.at[s], s * n_chunks)


def _rotary_tables(l):
    half = HEAD_DIM // 2
    inv_freq = ROPE_BASE ** (-jnp.arange(half, dtype=F32) / half)
    ang = jnp.arange(l, dtype=F32)[:, None] * inv_freq[None, :]
    cos, sin = jnp.cos(ang), jnp.sin(ang)
    return jnp.concatenate([cos, cos], axis=-1), jnp.concatenate([-sin, sin], axis=-1)


def _retention(proj, decay_exp, tok_heads):
    b, l, _ = proj.shape
    assert l % RET_CHUNK == 0 and (l // RET_CHUNK) % min(RET_UNROLL, l // RET_CHUNK) == 0
    dec = jnp.broadcast_to(decay_exp.astype(F32).T[:, :, None], (tok_heads, 2, HEAD_DIM))
    nb = _sequences_per_step(b, l)
    head_spec = lambda part: pl.BlockSpec((nb, l, HEAD_DIM), lambda bi, h: (bi, 0, part * tok_heads + h))
    n_chunks = l // RET_CHUNK
    return pl.pallas_call(
        functools.partial(_retention_kernel, n_chunks=n_chunks),
        grid=(b // nb, tok_heads),
        in_specs=[pl.BlockSpec((None, 2, HEAD_DIM), lambda bi, h: (h, 0, 0)),
                  head_spec(0), head_spec(1), head_spec(2), head_spec(3)],
        out_specs=pl.BlockSpec((nb, l, HEAD_DIM), lambda bi, h: (bi, 0, h)),
        out_shape=jax.ShapeDtypeStruct((b, l, tok_heads * HEAD_DIM), BF16),
        scratch_shapes=[pltpu.VMEM((nb * n_chunks, HEAD_DIM, HEAD_DIM), F32),
                        pltpu.VMEM((nb * n_chunks, HEAD_DIM, HEAD_DIM), BF16)],
        compiler_params=_params("parallel", "parallel"),
        name="retention",
    )(dec, proj, proj, proj, proj)


def _na_kernel(bias_ref, q_ref, k_ref, v_ref, o_ref, *, rows):
    win = NA_KH * GRID_W
    unroll = min(NA_UNROLL, rows)

    def one_sequence(q, k, v, o):
        def rows_step(i, carry):
            qrows = [i * unroll + u for u in range(unroll)]
            starts = [jnp.clip(r - NA_KH // 2, 0, rows - NA_KH) for r in qrows]
            q_offs = [pl.multiple_of(r * GRID_W, GRID_W) for r in qrows]
            k_offs = [pl.multiple_of(rs * GRID_W, GRID_W) for rs in starts]
            scores = [lax.dot_general(q[pl.ds(q_off, GRID_W), :], k[pl.ds(k_off, win), :], NT_DIMS,
                                      preferred_element_type=F32)
                      for q_off, k_off in zip(q_offs, k_offs)]
            probs, denoms = [], []
            for r, rs, s in zip(qrows, starts, scores):
                first = rs - r + NA_KH - 1
                s = s + jnp.concatenate([bias_ref[first + 2 * j] for j in range(NA_KH // 2)], axis=1)
                p = jnp.exp2(s - jnp.max(s, axis=-1, keepdims=True))
                denoms.append(jnp.sum(p, axis=-1, keepdims=True))
                probs.append(p.astype(BF16))
            outs = [jnp.dot(p, v[pl.ds(k_off, win), :], preferred_element_type=F32)
                    for p, k_off in zip(probs, k_offs)]
            for q_off, out, denom in zip(q_offs, outs, denoms):
                o[pl.ds(q_off, GRID_W), :] = (out / denom).astype(o.dtype)
            return carry

        lax.fori_loop(0, rows // unroll, rows_step, 0)

    for s in range(q_ref.shape[0]):
        one_sequence(q_ref.at[s], k_ref.at[s], v_ref.at[s], o_ref.at[s])


def _na_bias_table(rpb):
    c = np.arange(GRID_W)
    win_start = np.clip(c - NA_KW // 2, 0, GRID_W - NA_KW)
    valid = (c[None, :] >= win_start[:, None]) & (c[None, :] < win_start[:, None] + NA_KW)
    pad = GRID_W - NA_KW
    padded = jnp.pad(rpb.astype(F32), ((0, 0), (0, 0), (pad, pad)))
    toeplitz = jnp.stack([padded[:, :, GRID_W - 1 - ci:2 * GRID_W - 1 - ci] for ci in range(GRID_W)], axis=2)
    toeplitz = jnp.where(valid[None, None], toeplitz * LOG2_E, -jnp.inf)
    return jnp.concatenate([toeplitz[:, :-1], toeplitz[:, 1:]], axis=-1)


def _neighbourhood(proj, rpb, tok_heads):
    b, l, _ = proj.shape
    rows = l // GRID_W
    assert l % GRID_W == 0 and rows >= NA_KH and rows % min(NA_UNROLL, rows) == 0
    bias = _na_bias_table(rpb)
    nb = _sequences_per_step(b, l)
    head_spec = lambda part: pl.BlockSpec((nb, l, HEAD_DIM), lambda bi, h: (bi, 0, part * tok_heads + h))
    return pl.pallas_call(
        functools.partial(_na_kernel, rows=rows),
        grid=(b // nb, tok_heads),
        in_specs=[pl.BlockSpec((None, 2 * NA_KH - 2, GRID_W, 2 * GRID_W), lambda bi, h: (h, 0, 0, 0)),
                  head_spec(0), head_spec(1), head_spec(2)],
        out_specs=pl.BlockSpec((nb, l, HEAD_DIM), lambda bi, h: (bi, 0, h)),
        out_shape=jax.ShapeDtypeStruct((b, l, tok_heads * HEAD_DIM), BF16),
        compiler_params=_params("parallel", "parallel"),
        name="neighbourhood",
    )(bias, proj, proj, proj)


def _trunk(x, mem, norm_gain, mem_norm_gain, w_mem_kv, w_out, w_mlp_in, w_mlp_out, w_in_ret, ret_decay,
           w_in_na, na_rpb):
    b, l, d = x.shape
    n_mem = mem.shape[1]
    tok_width = w_out.shape[1] - MEM_WIDTH
    tok_heads = tok_width // HEAD_DIM
    xf = x.reshape(b * l, d)
    memf = mem.reshape(b * n_mem, d)
    for i in range(norm_gain.shape[0]):
        g = norm_gain[i]
        mem_kv = _norm_matmul(memf, mem_norm_gain[i], w_mem_kv, i).reshape(b, n_mem, 2 * MEM_WIDTH)
        j = i // N_MIXERS
        if i % N_MIXERS == 0:
            proj = _norm_matmul(xf, g[0], w_in_ret, j, columns="retention", tok_width=tok_width,
                                seq=l).reshape(b, l, -1)
            tok = _retention(proj, ret_decay[j], tok_heads)
        else:
            proj = _norm_matmul(xf, g[0], w_in_na, j, columns="neighbourhood", tok_width=tok_width,
                                seq=l).reshape(b, l, -1)
            tok = _neighbourhood(proj, na_rpb[j], tok_heads)
        x1 = _out_proj(tok.reshape(b * l, tok_width), proj.reshape(b * l, -1), mem_kv, xf, g[1], w_out, i)
        xf = _mlp(x1, g[2], g[3], w_mlp_in, w_mlp_out, i)
    return xf.reshape(b, l, d)


def kernel(x_prompt, x_sample, mem_prompt, mem_sample, norm_gain, mem_norm_gain, w_mem_kv, w_out, w_mlp_in,
           w_mlp_out, w_in_ret, ret_decay, w_in_na, na_rpb):
    weights = [w.astype(BF16) for w in (w_mem_kv, w_out, w_mlp_in, w_mlp_out, w_in_ret)]
    w_mem_kv, w_out, w_mlp_in, w_mlp_out, w_in_ret = weights
    w_in_na = w_in_na.astype(BF16)
    run = functools.partial(_trunk, norm_gain=norm_gain, mem_norm_gain=mem_norm_gain, w_mem_kv=w_mem_kv,
                            w_out=w_out, w_mlp_in=w_mlp_in, w_mlp_out=w_mlp_out, w_in_ret=w_in_ret,
                            ret_decay=ret_decay, w_in_na=w_in_na, na_rpb=na_rpb)
    return (run(x_prompt, mem_prompt), run(x_sample, mem_sample))
```

```python
import functools

import jax
import jax.numpy as jnp
import numpy as np
from jax import lax
from jax.experimental import pallas as pl
from jax.experimental.pallas import tpu as pltpu

HEAD_DIM = 128
MEM_HEADS = 4
MEM_WIDTH = MEM_HEADS * HEAD_DIM
N_MIXERS = 2
RET_CHUNK = 128
ROPE_BASE = 10000.0
GRID_W = 64
NA_KH = 8
NA_KW = 16
NORM_EPS = 1e-6
ATTN_SCALE = HEAD_DIM ** -0.5
LOG2_E = 1.4426950408889634
MLP_HIDDEN_TILE = 1024
MIXER_STEP_TOKENS = 8192
RET_UNROLL = 64
NA_UNROLL = 64

V7X_VMEM_BYTES = 64 * 1024 * 1024
VMEM_LIMIT_BYTES = V7X_VMEM_BYTES - 4 * 1024 * 1024

F32 = jnp.float32
BF16 = jnp.bfloat16
NT_DIMS = (((1,), (1,)), ((), ()))


def _params(*semantics):
    return pltpu.CompilerParams(dimension_semantics=semantics, vmem_limit_bytes=VMEM_LIMIT_BYTES)


def _rms(x, gain):
    return x * lax.rsqrt(jnp.mean(x * x, axis=-1, keepdims=True) + NORM_EPS) * gain


def _rotate(x, cos2, sin2):
    return x * cos2 + pltpu.roll(x, HEAD_DIM // 2, axis=1) * sin2


def _retention_columns(res, cos2, sin2, tok_width):
    heads = tok_width // HEAD_DIM
    blocks = []
    for c in range(res.shape[1] // HEAD_DIM):
        blk = res[:, c * HEAD_DIM:(c + 1) * HEAD_DIM]
        part = c // heads
        if part == 0:
            blk = _rotate(blk, cos2, sin2)
        elif part == 1:
            blk = _rotate(blk, cos2, sin2) * ATTN_SCALE
        elif part == 3:
            blk = blk * (1.0 / (1.0 + jnp.exp(-blk)))
        blocks.append(blk.astype(BF16))
    return jnp.concatenate(blocks, axis=1)


def _neighbourhood_columns(res, tok_width):
    q = res[:, :tok_width] * (ATTN_SCALE * LOG2_E)
    return jnp.concatenate([q.astype(BF16), res[:, tok_width:].astype(BF16)], axis=1)


def _norm_matmul_kernel(x_ref, g_ref, w_ref, *rest, sub_rows, columns, tok_width):
    o_ref = rest[-1]
    gain = g_ref[...]
    for r in range(x_ref.shape[0] // sub_rows):
        rows = slice(r * sub_rows, (r + 1) * sub_rows)
        h = _rms(x_ref[rows, :], gain).astype(BF16)
        res = jnp.dot(h, w_ref[...], preferred_element_type=F32)
        if columns == "retention":
            cos_ref, sin_ref = rest[0], rest[1]
            res = _retention_columns(res, cos_ref[rows, :], sin_ref[rows, :], tok_width)
        elif columns == "neighbourhood":
            res = _neighbourhood_columns(res, tok_width)
        o_ref[rows, :] = res.astype(o_ref.dtype)


def _norm_matmul(x, gain, w, layer, *, columns=None, tok_width=None, seq=None, tm=512, sub_rows=256):
    t, d = x.shape
    n = w.shape[2]
    tm = min(tm, t if seq is None else seq)
    sub_rows = min(sub_rows, tm)
    assert t % tm == 0 and tm % sub_rows == 0
    operands = [x, gain.reshape(1, d), w]
    in_specs = [
        pl.BlockSpec((tm, d), lambda i: (i, 0)),
        pl.BlockSpec((1, d), lambda i: (0, 0)),
        pl.BlockSpec((None, d, n), lambda i: (layer, 0, 0), pipeline_mode=pl.Buffered(1)),
    ]
    if columns == "retention":
        assert seq % tm == 0
        tiles_per_seq = seq // tm
        operands += list(_rotary_tables(seq))
        in_specs += [pl.BlockSpec((tm, HEAD_DIM), lambda i: (i % tiles_per_seq, 0)),
                     pl.BlockSpec((tm, HEAD_DIM), lambda i: (i % tiles_per_seq, 0))]
    return pl.pallas_call(
        functools.partial(_norm_matmul_kernel, sub_rows=sub_rows, columns=columns, tok_width=tok_width),
        grid=(t // tm,),
        in_specs=in_specs,
        out_specs=pl.BlockSpec((tm, n), lambda i: (i, 0)),
        out_shape=jax.ShapeDtypeStruct((t, n), BF16),
        compiler_params=_params("parallel"),
        name="norm_matmul",
    )(*operands)


def _mem_scores(q, k_ref):
    return [lax.dot_general(q[:, h * HEAD_DIM:(h + 1) * HEAD_DIM], k_ref[:, h * HEAD_DIM:(h + 1) * HEAD_DIM],
                            NT_DIMS, preferred_element_type=F32) for h in range(MEM_HEADS)]


def _mem_values(scores, v_ref):
    probs, denoms = [], []
    for s in scores:
        s = s * ATTN_SCALE
        p = jnp.exp(s - jnp.max(s, axis=-1, keepdims=True))
        denoms.append(jnp.sum(p, axis=-1, keepdims=True))
        probs.append(p.astype(BF16))
    outs = [jnp.dot(p, v_ref[:, h * HEAD_DIM:(h + 1) * HEAD_DIM], preferred_element_type=F32)
            for h, p in enumerate(probs)]
    return jnp.concatenate([(o / dn).astype(BF16) for o, dn in zip(outs, denoms)], axis=1)


def _out_proj_kernel(tok_ref, qm_ref, k_ref, v_ref, x_ref, g_ref, wt_ref, wm_ref, o_ref, *, sub_rows):
    gain = g_ref[...]
    n_sub = x_ref.shape[0] // sub_rows
    rows = [slice(r * sub_rows, (r + 1) * sub_rows) for r in range(n_sub)]
    mem = _mem_values(_mem_scores(qm_ref[rows[0], :], k_ref), v_ref)
    for r in range(n_sub):
        if r + 1 < n_sub:
            next_scores = _mem_scores(qm_ref[rows[r + 1], :], k_ref)
        mixed = (jnp.dot(tok_ref[rows[r], :], wt_ref[...], preferred_element_type=F32)
                 + jnp.dot(mem, wm_ref[...], preferred_element_type=F32))
        if r + 1 < n_sub:
            mem = _mem_values(next_scores, v_ref)
        o_ref[rows[r], :] = x_ref[rows[r], :] + _rms(mixed, gain)


def _out_proj(tok, proj, mem_kv, x, gain, w_out, layer, *, tm=1024, sub_rows=256):
    t, d = x.shape
    b, m, _ = mem_kv.shape
    seq = t // b
    tm = min(tm, seq)
    sub_rows = min(sub_rows, tm)
    tw, n = tok.shape[1], proj.shape[1]
    assert seq % tm == 0 and tm % sub_rows == 0 and tw % MEM_WIDTH == 0 and n % MEM_WIDTH == 0
    assert w_out.shape[1] == tw + MEM_WIDTH
    tiles_per_seq = seq // tm
    q_block = n // MEM_WIDTH - 1
    return pl.pallas_call(
        functools.partial(_out_proj_kernel, sub_rows=sub_rows),
        grid=(t // tm,),
        in_specs=[
            pl.BlockSpec((tm, tw), lambda i: (i, 0)),
            pl.BlockSpec((tm, MEM_WIDTH), lambda i: (i, q_block)),
            pl.BlockSpec((None, m, MEM_WIDTH), lambda i: (i // tiles_per_seq, 0, 0)),
            pl.BlockSpec((None, m, MEM_WIDTH), lambda i: (i // tiles_per_seq, 0, 1)),
            pl.BlockSpec((tm, d), lambda i: (i, 0)),
            pl.BlockSpec((1, d), lambda i: (0, 0)),
            pl.BlockSpec((None, tw, d), lambda i: (layer, 0, 0), pipeline_mode=pl.Buffered(1)),
            pl.BlockSpec((None, MEM_WIDTH, d), lambda i: (layer, tw // MEM_WIDTH, 0),
                         pipeline_mode=pl.Buffered(1)),
        ],
        out_specs=pl.BlockSpec((tm, d), lambda i: (i, 0)),
        out_shape=jax.ShapeDtypeStruct((t, d), F32),
        compiler_params=_params("parallel"),
        name="out_proj",
    )(tok, proj, mem_kv, mem_kv, x, gain.reshape(1, d), w_out, w_out)


def _mlp_kernel(x_ref, gi_ref, go_ref, w1_ref, w2_ref, o_ref, h_ref, *, sub_rows):
    j = pl.program_id(1)
    last = pl.num_programs(1) - 1
    n_sub = x_ref.shape[0] // sub_rows

    def step(first, final):
        hidden = []
        for r in range(n_sub):
            rows = slice(r * sub_rows, (r + 1) * sub_rows)
            if first:
                h = _rms(x_ref[rows, :], gi_ref[...]).astype(BF16)
                h_ref[rows, :] = h
            else:
                h = h_ref[rows, :]
            u = jnp.dot(h, w1_ref[...], preferred_element_type=F32)
            hidden.append(jnp.square(jnp.maximum(u, 0.0)).astype(BF16))
        for r in range(n_sub):
            rows = slice(r * sub_rows, (r + 1) * sub_rows)
            acc = jnp.dot(hidden[r], w2_ref[...], preferred_element_type=F32)
            if not first:
                acc = o_ref[rows, :] + acc
            if final:
                acc = x_ref[rows, :] + _rms(acc, go_ref[...])
            o_ref[rows, :] = acc

    pl.when(j == 0)(lambda: step(True, False))
    pl.when(jnp.logical_and(j > 0, j < last))(lambda: step(False, False))
    pl.when(j == last)(lambda: step(False, True))


def _mlp(x, gain_in, gain_out, w1, w2, layer, *, tm=1024, tf=MLP_HIDDEN_TILE, sub_rows=512):
    t, d = x.shape
    f = w1.shape[2]
    tm = min(tm, t)
    sub_rows = min(sub_rows, tm)
    assert t % tm == 0 and tm % sub_rows == 0 and f % tf == 0 and f // tf >= 2
    return pl.pallas_call(
        functools.partial(_mlp_kernel, sub_rows=sub_rows),
        grid=(t // tm, f // tf),
        in_specs=[
            pl.BlockSpec((tm, d), lambda i, j: (i, 0)),
            pl.BlockSpec((1, d), lambda i, j: (0, 0)),
            pl.BlockSpec((1, d), lambda i, j: (0, 0)),
            pl.BlockSpec((None, d, tf), lambda i, j: (layer, 0, j)),
            pl.BlockSpec((None, tf, d), lambda i, j: (layer, j, 0)),
        ],
        out_specs=pl.BlockSpec((tm, d), lambda i, j: (i, 0)),
        out_shape=jax.ShapeDtypeStruct((t, d), F32),
        scratch_shapes=[pltpu.VMEM((tm, d), BF16)],
        compiler_params=_params("parallel", "arbitrary"),
        name="mlp",
    )(x, gain_in.reshape(1, d), gain_out.reshape(1, d), w1, w2)


def _sequences_per_step(b, l):
    nb = max(1, min(b, MIXER_STEP_TOKENS // l))
    while b % nb:
        nb -= 1
    return nb


def _retention_kernel(dec_ref, q_ref, k_ref, v_ref, g_ref, o_ref, kvb_ref, sf_ref, *, n_chunks):
    c = RET_CHUNK
    unroll = min(RET_UNROLL, n_chunks)
    log_gamma = jnp.log1p(-jnp.exp2(-dec_ref[...]))
    lgf, lgb = log_gamma[0:1, :], log_gamma[1:2, :]
    lane = lax.broadcasted_iota(jnp.int32, (1, c), 1).astype(F32)
    k_decay_f_row = jnp.exp((c - 1.0 - lane) * lgf).astype(BF16)
    k_decay_b_row = jnp.exp(lane * lgb).astype(BF16)
    chunk_decay_f = jnp.exp(c * lgf)
    chunk_decay_b = jnp.exp(c * lgb)
    row = lax.broadcasted_iota(jnp.int32, (c, c), 0).astype(F32)
    col = lax.broadcasted_iota(jnp.int32, (c, c), 1).astype(F32)
    diff = row - col
    decay_intra = jnp.exp(jnp.where(diff >= 0, diff * lgf, -diff * lgb))
    pos = lax.broadcasted_iota(jnp.int32, (c, HEAD_DIM), 0).astype(F32)
    q_decay_f = jnp.exp((pos + 1.0) * lgf).astype(BF16)
    q_decay_b = jnp.exp((c - pos) * lgb).astype(BF16)
    zero_state = jnp.zeros((HEAD_DIM, HEAD_DIM), F32)

    def one_sequence(q, k, v, g, o, base):
        def forward_group(i, sf):
            chunks = [i * unroll + u for u in range(unroll)]
            offs = [pl.multiple_of(n * c, c) for n in chunks]
            lhs = []
            for off in offs:
                kr_t = k[pl.ds(off, c), :].T
                lhs.append(jnp.concatenate([kr_t * k_decay_f_row, kr_t * k_decay_b_row], axis=0))
            kvs = [jnp.dot(lhs[u], v[pl.ds(off, c), :], preferred_element_type=F32)
                   for u, off in enumerate(offs)]
            for n, kv in zip(chunks, kvs):
                sf_ref[base + n] = sf.astype(BF16)
                sf = chunk_decay_f * sf + kv[0:HEAD_DIM, :]
                kvb_ref[base + n] = kv[HEAD_DIM:2 * HEAD_DIM, :]
            return sf

        lax.fori_loop(0, n_chunks // unroll, forward_group, zero_state)

        def backward_group(t, sb):
            i = n_chunks // unroll - 1 - t
            chunks = [i * unroll + u for u in range(unroll)]
            offs = [pl.multiple_of(n * c, c) for n in chunks]
            scores = [lax.dot_general(q[pl.ds(off, c), :], k[pl.ds(off, c), :], NT_DIMS,
                                      preferred_element_type=F32) for off in offs]
            states = [None] * unroll
            for u in reversed(range(unroll)):
                states[u] = sb.astype(BF16)
                sb = chunk_decay_b * sb + kvb_ref[base + chunks[u]]
            ys = []
            for n, off, s, sb_n in zip(chunks, offs, scores, states):
                qr = q[pl.ds(off, c), :]
                lhs = jnp.concatenate([(s * decay_intra).astype(BF16), qr * q_decay_f, qr * q_decay_b], axis=1)
                rhs = jnp.concatenate([v[pl.ds(off, c), :], sf_ref[base + n], sb_n], axis=0)
                ys.append(jnp.dot(lhs, rhs, preferred_element_type=F32))
            for off, y in zip(offs, ys):
                y = y * lax.rsqrt(jnp.mean(y * y, axis=-1, keepdims=True) + NORM_EPS)
                o[pl.ds(off, c), :] = (g[pl.ds(off, c), :].astype(F32) * y).astype(o.dtype)
            return sb

        lax.fori_loop(0, n_chunks // unroll, backward_group, zero_state)

    for s in range(q_ref.shape[0]):
        one_sequence(q_ref.at[s], k_ref.at[s], v_ref.at[s], g_ref.at[s], o_ref.at[s], s * n_chunks)


def _rotary_tables(l):
    half = HEAD_DIM // 2
    inv_freq = ROPE_BASE ** (-jnp.arange(half, dtype=F32) / half)
    ang = jnp.arange(l, dtype=F32)[:, None] * inv_freq[None, :]
    cos, sin = jnp.cos(ang), jnp.sin(ang)
    return jnp.concatenate([cos, cos], axis=-1), jnp.concatenate([-sin, sin], axis=-1)


def _retention(proj, decay_exp, tok_heads):
    b, l, _ = proj.shape
    assert l % RET_CHUNK == 0 and (l // RET_CHUNK) % min(RET_UNROLL, l // RET_CHUNK) == 0
    dec = jnp.broadcast_to(decay_exp.astype(F32).T[:, :, None], (tok_heads, 2, HEAD_DIM))
    nb = _sequences_per_step(b, l)
    head_spec = lambda part: pl.BlockSpec((nb, l, HEAD_DIM), lambda bi, h: (bi, 0, part * tok_heads + h))
    n_chunks = l // RET_CHUNK
    return pl.pallas_call(
        functools.partial(_retention_kernel, n_chunks=n_chunks),
        grid=(b // nb, tok_heads),
        in_specs=[pl.BlockSpec((None, 2, HEAD_DIM), lambda bi, h: (h, 0, 0)),
                  head_spec(0), head_spec(1), head_spec(2), head_spec(3)],
        out_specs=pl.BlockSpec((nb, l, HEAD_DIM), lambda bi, h: (bi, 0, h)),
        out_shape=jax.ShapeDtypeStruct((b, l, tok_heads * HEAD_DIM), BF16),
        scratch_shapes=[pltpu.VMEM((nb * n_chunks, HEAD_DIM, HEAD_DIM), F32),
                        pltpu.VMEM((nb * n_chunks, HEAD_DIM, HEAD_DIM), BF16)],
        compiler_params=_params("parallel", "parallel"),
        name="retention",
    )(dec, proj, proj, proj, proj)


def _na_kernel(bias_ref, q_ref, k_ref, v_ref, o_ref, *, rows):
    win = NA_KH * GRID_W
    unroll = min(NA_UNROLL, rows)

    def one_sequence(q, k, v, o):
        def rows_step(i, carry):
            qrows = [i * unroll + u for u in range(unroll)]
            starts = [jnp.clip(r - NA_KH // 2, 0, rows - NA_KH) for r in qrows]
            q_offs = [pl.multiple_of(r * GRID_W, GRID_W) for r in qrows]
            k_offs = [pl.multiple_of(rs * GRID_W, GRID_W) for rs in starts]
            scores = [lax.dot_general(q[pl.ds(q_off, GRID_W), :], k[pl.ds(k_off, win), :], NT_DIMS,
                                      preferred_element_type=F32)
                      for q_off, k_off in zip(q_offs, k_offs)]
            probs, denoms = [], []
            for r, rs, s in zip(qrows, starts, scores):
                first = rs - r + NA_KH - 1
                s = s + jnp.concatenate([bias_ref[first + 2 * j] for j in range(NA_KH // 2)], axis=1)
                p = jnp.exp2(s - jnp.max(s, axis=-1, keepdims=True))
                denoms.append(jnp.sum(p, axis=-1, keepdims=True))
                probs.append(p.astype(BF16))
            outs = [jnp.dot(p, v[pl.ds(k_off, win), :], preferred_element_type=F32)
                    for p, k_off in zip(probs, k_offs)]
            for q_off, out, denom in zip(q_offs, outs, denoms):
                o[pl.ds(q_off, GRID_W), :] = (out / denom).astype(o.dtype)
            return carry

        lax.fori_loop(0, rows // unroll, rows_step, 0)

    for s in range(q_ref.shape[0]):
        one_sequence(q_ref.at[s], k_ref.at[s], v_ref.at[s], o_ref.at[s])


def _na_bias_table(rpb):
    c = np.arange(GRID_W)
    win_start = np.clip(c - NA_KW // 2, 0, GRID_W - NA_KW)
    valid = (c[None, :] >= win_start[:, None]) & (c[None, :] < win_start[:, None] + NA_KW)
    pad = GRID_W - NA_KW
    padded = jnp.pad(rpb.astype(F32), ((0, 0), (0, 0), (pad, pad)))
    toeplitz = jnp.stack([padded[:, :, GRID_W - 1 - ci:2 * GRID_W - 1 - ci] for ci in range(GRID_W)], axis=2)
    toeplitz = jnp.where(valid[None, None], toeplitz * LOG2_E, -jnp.inf)
    return jnp.concatenate([toeplitz[:, :-1], toeplitz[:, 1:]], axis=-1)


def _neighbourhood(proj, rpb, tok_heads):
    b, l, _ = proj.shape
    rows = l // GRID_W
    assert l % GRID_W == 0 and rows >= NA_KH and rows % min(NA_UNROLL, rows) == 0
    bias = _na_bias_table(rpb)
    nb = _sequences_per_step(b, l)
    head_spec = lambda part: pl.BlockSpec((nb, l, HEAD_DIM), lambda bi, h: (bi, 0, part * tok_heads + h))
    return pl.pallas_call(
        functools.partial(_na_kernel, rows=rows),
        grid=(b // nb, tok_heads),
        in_specs=[pl.BlockSpec((None, 2 * NA_KH - 2, GRID_W, 2 * GRID_W), lambda bi, h: (h, 0, 0, 0)),
                  head_spec(0), head_spec(1), head_spec(2)],
        out_specs=pl.BlockSpec((nb, l, HEAD_DIM), lambda bi, h: (bi, 0, h)),
        out_shape=jax.ShapeDtypeStruct((b, l, tok_heads * HEAD_DIM), BF16),
        compiler_params=_params("parallel", "parallel"),
        name="neighbourhood",
    )(bias, proj, proj, proj)


def _trunk(x, mem, norm_gain, mem_norm_gain, w_mem_kv, w_out, w_mlp_in, w_mlp_out, w_in_ret, ret_decay,
           w_in_na, na_rpb):
    b, l, d = x.shape
    n_mem = mem.shape[1]
    tok_width = w_out.shape[1] - MEM_WIDTH
    tok_heads = tok_width // HEAD_DIM
    xf = x.reshape(b * l, d)
    memf = mem.reshape(b * n_mem, d)
    for i in range(norm_gain.shape[0]):
        g = norm_gain[i]
        mem_kv = _norm_matmul(memf, mem_norm_gain[i], w_mem_kv, i).reshape(b, n_mem, 2 * MEM_WIDTH)
        j = i // N_MIXERS
        if i % N_MIXERS == 0:
            proj = _norm_matmul(xf, g[0], w_in_ret, j, columns="retention", tok_width=tok_width,
                                seq=l).reshape(b, l, -1)
            tok = _retention(proj, ret_decay[j], tok_heads)
        else:
            proj = _norm_matmul(xf, g[0], w_in_na, j, columns="neighbourhood", tok_width=tok_width,
                                seq=l).reshape(b, l, -1)
            tok = _neighbourhood(proj, na_rpb[j], tok_heads)
        x1 = _out_proj(tok.reshape(b * l, tok_width), proj.reshape(b * l, -1), mem_kv, xf, g[1], w_out, i)
        xf = _mlp(x1, g[2], g[3], w_mlp_in, w_mlp_out, i)
    return xf.reshape(b, l, d)


def kernel(x_prompt, x_sample, mem_prompt, mem_sample, norm_gain, mem_norm_gain, w_mem_kv, w_out, w_mlp_in,
           w_mlp_out, w_in_ret, ret_decay, w_in_na, na_rpb):
    weights = [w.astype(BF16) for w in (w_mem_kv, w_out, w_mlp_in, w_mlp_out, w_in_ret)]
    w_mem_kv, w_out, w_mlp_in, w_mlp_out, w_in_ret = weights
    w_in_na = w_in_na.astype(BF16)
    run = functools.partial(_trunk, norm_gain=norm_gain, mem_norm_gain=mem_norm_gain, w_mem_kv=w_mem_kv,
                            w_out=w_out, w_mlp_in=w_mlp_in, w_mlp_out=w_mlp_out, w_in_ret=w_in_ret,
                            ret_decay=ret_decay, w_in_na=w_in_na, na_rpb=na_rpb)
    return (run(x_prompt, mem_prompt), run(x_sample, mem_sample))
```

```python
import functools

import jax
import jax.numpy as jnp
import numpy as np
from jax import lax
from jax.experimental import pallas as pl
from jax.experimental.pallas import tpu as pltpu

HEAD_DIM = 128
MEM_HEADS = 4
MEM_WIDTH = MEM_HEADS * HEAD_DIM
N_MIXERS = 2
RET_CHUNK = 128
ROPE_BASE = 10000.0
GRID_W = 64
NA_KH = 8
NA_KW = 16
NORM_EPS = 1e-6
ATTN_SCALE = HEAD_DIM ** -0.5
LOG2_E = 1.4426950408889634
MLP_HIDDEN_TILE = 1024
MIXER_STEP_TOKENS = 8192
RET_UNROLL = 64
NA_UNROLL = 64

V7X_VMEM_BYTES = 64 * 1024 * 1024
VMEM_LIMIT_BYTES = V7X_VMEM_BYTES - 4 * 1024 * 1024

F32 = jnp.float32
BF16 = jnp.bfloat16
NT_DIMS = (((1,), (1,)), ((), ()))


def _params(*semantics):
    return pltpu.CompilerParams(dimension_semantics=semantics, vmem_limit_bytes=VMEM_LIMIT_BYTES)


def _rms(x, gain):
    return x * lax.rsqrt(jnp.mean(x * x, axis=-1, keepdims=True) + NORM_EPS) * gain


def _rotate(x, cos2, sin2):
    return x * cos2 + pltpu.roll(x, HEAD_DIM // 2, axis=1) * sin2


def _retention_columns(res, cos2, sin2, tok_width):
    heads = tok_width // HEAD_DIM
    blocks = []
    for c in range(res.shape[1] // HEAD_DIM):
        blk = res[:, c * HEAD_DIM:(c + 1) * HEAD_DIM]
        part = c // heads
        if part == 0:
            blk = _rotate(blk, cos2, sin2)
        elif part == 1:
            blk = _rotate(blk, cos2, sin2) * ATTN_SCALE
        elif part == 3:
            blk = blk * (1.0 / (1.0 + jnp.exp(-blk)))
        blocks.append(blk.astype(BF16))
    return jnp.concatenate(blocks, axis=1)


def _neighbourhood_columns(res, tok_width):
    q = res[:, :tok_width] * (ATTN_SCALE * LOG2_E)
    return jnp.concatenate([q.astype(BF16), res[:, tok_width:].astype(BF16)], axis=1)


def _norm_matmul_kernel(x_ref, g_ref, w_ref, *rest, sub_rows, columns, tok_width):
    o_ref = rest[-1]
    gain = g_ref[...]
    for r in range(x_ref.shape[0] // sub_rows):
        rows = slice(r * sub_rows, (r + 1) * sub_rows)
        h = _rms(x_ref[rows, :], gain).astype(BF16)
        res = jnp.dot(h, w_ref[...], preferred_element_type=F32)
        if columns == "retention":
            cos_ref, sin_ref = rest[0], rest[1]
            res = _retention_columns(res, cos_ref[rows, :], sin_ref[rows, :], tok_width)
        elif columns == "neighbourhood":
            res = _neighbourhood_columns(res, tok_width)
        o_ref[rows, :] = res.astype(o_ref.dtype)


def _norm_matmul(x, gain, w, layer, *, columns=None, tok_width=None, seq=None, tm=512, sub_rows=256):
    t, d = x.shape
    n = w.shape[2]
    tm = min(tm, t if seq is None else seq)
    sub_rows = min(sub_rows, tm)
    assert t % tm == 0 and tm % sub_rows == 0
    operands = [x, gain.reshape(1, d), w]
    in_specs = [
        pl.BlockSpec((tm, d), lambda i: (i, 0)),
        pl.BlockSpec((1, d), lambda i: (0, 0)),
        pl.BlockSpec((None, d, n), lambda i: (layer, 0, 0), pipeline_mode=pl.Buffered(1)),
    ]
    if columns == "retention":
        assert seq % tm == 0
        tiles_per_seq = seq // tm
        operands += list(_rotary_tables(seq))
        in_specs += [pl.BlockSpec((tm, HEAD_DIM), lambda i: (i % tiles_per_seq, 0)),
                     pl.BlockSpec((tm, HEAD_DIM), lambda i: (i % tiles_per_seq, 0))]
    return pl.pallas_call(
        functools.partial(_norm_matmul_kernel, sub_rows=sub_rows, columns=columns, tok_width=tok_width),
        grid=(t // tm,),
        in_specs=in_specs,
        out_specs=pl.BlockSpec((tm, n), lambda i: (i, 0)),
        out_shape=jax.ShapeDtypeStruct((t, n), BF16),
        compiler_params=_params("parallel"),
        name="norm_matmul",
    )(*operands)


def _mem_scores(q, k_ref):
    return [lax.dot_general(q[:, h * HEAD_DIM:(h + 1) * HEAD_DIM], k_ref[:, h * HEAD_DIM:(h + 1) * HEAD_DIM],
                            NT_DIMS, preferred_element_type=F32) for h in range(MEM_HEADS)]


def _mem_values(scores, v_ref):
    probs, denoms = [], []
    for s in scores:
        s = s * ATTN_SCALE
        p = jnp.exp(s - jnp.max(s, axis=-1, keepdims=True))
        denoms.append(jnp.sum(p, axis=-1, keepdims=True))
        probs.append(p.astype(BF16))
    outs = [jnp.dot(p, v_ref[:, h * HEAD_DIM:(h + 1) * HEAD_DIM], preferred_element_type=F32)
            for h, p in enumerate(probs)]
    return jnp.concatenate([(o / dn).astype(BF16) for o, dn in zip(outs, denoms)], axis=1)


def _out_proj_kernel(tok_ref, qm_ref, k_ref, v_ref, x_ref, g_ref, wt_ref, wm_ref, o_ref, *, sub_rows):
    gain = g_ref[...]
    n_sub = x_ref.shape[0] // sub_rows
    rows = [slice(r * sub_rows, (r + 1) * sub_rows) for r in range(n_sub)]
    mem = _mem_values(_mem_scores(qm_ref[rows[0], :], k_ref), v_ref)
    for r in range(n_sub):
        if r + 1 < n_sub:
            next_scores = _mem_scores(qm_ref[rows[r + 1], :], k_ref)
        mixed = (jnp.dot(tok_ref[rows[r], :], wt_ref[...], preferred_element_type=F32)
                 + jnp.dot(mem, wm_ref[...], preferred_element_type=F32))
        if r + 1 < n_sub:
            mem = _mem_values(next_scores, v_ref)
        o_ref[rows[r], :] = x_ref[rows[r], :] + _rms(mixed, gain)


def _out_proj(tok, proj, mem_kv, x, gain, w_out, layer, *, tm=1024, sub_rows=256):
    t, d = x.shape
    b, m, _ = mem_kv.shape
    seq = t // b
    tm = min(tm, seq)
    sub_rows = min(sub_rows, tm)
    tw, n = tok.shape[1], proj.shape[1]
    assert seq % tm == 0 and tm % sub_rows == 0 and tw % MEM_WIDTH == 0 and n % MEM_WIDTH == 0
    assert w_out.shape[1] == tw + MEM_WIDTH
    tiles_per_seq = seq // tm
    q_block = n // MEM_WIDTH - 1
    return pl.pallas_call(
        functools.partial(_out_proj_kernel, sub_rows=sub_rows),
        grid=(t // tm,),
        in_specs=[
            pl.BlockSpec((tm, tw), lambda i: (i, 0)),
            pl.BlockSpec((tm, MEM_WIDTH), lambda i: (i, q_block)),
            pl.BlockSpec((None, m, MEM_WIDTH), lambda i: (i // tiles_per_seq, 0, 0)),
            pl.BlockSpec((None, m, MEM_WIDTH), lambda i: (i // tiles_per_seq, 0, 1)),
            pl.BlockSpec((tm, d), lambda i: (i, 0)),
            pl.BlockSpec((1, d), lambda i: (0, 0)),
            pl.BlockSpec((None, tw, d), lambda i: (layer, 0, 0), pipeline_mode=pl.Buffered(1)),
            pl.BlockSpec((None, MEM_WIDTH, d), lambda i: (layer, tw // MEM_WIDTH, 0),
                         pipeline_mode=pl.Buffered(1)),
        ],
        out_specs=pl.BlockSpec((tm, d), lambda i: (i, 0)),
        out_shape=jax.ShapeDtypeStruct((t, d), F32),
        compiler_params=_params("parallel"),
        name="out_proj",
    )(tok, proj, mem_kv, mem_kv, x, gain.reshape(1, d), w_out, w_out)


def _mlp_kernel(x_ref, gi_ref, go_ref, w1_ref, w2_ref, o_ref, h_ref, *, sub_rows):
    j = pl.program_id(1)
    last = pl.num_programs(1) - 1
    n_sub = x_ref.shape[0] // sub_rows

    def step(first, final):
        hidden = []
        for r in range(n_sub):
            rows = slice(r * sub_rows, (r + 1) * sub_rows)
            if first:
                h = _rms(x_ref[rows, :], gi_ref[...]).astype(BF16)
                h_ref[rows, :] = h
            else:
                h = h_ref[rows, :]
            u = jnp.dot(h, w1_ref[...], preferred_element_type=F32)
            hidden.append(jnp.square(jnp.maximum(u, 0.0)).astype(BF16))
        for r in range(n_sub):
            rows = slice(r * sub_rows, (r + 1) * sub_rows)
            acc = jnp.dot(hidden[r], w2_ref[...], preferred_element_type=F32)
            if not first:
                acc = o_ref[rows, :] + acc
            if final:
                acc = x_ref[rows, :] + _rms(acc, go_ref[...])
            o_ref[rows, :] = acc

    pl.when(j == 0)(lambda: step(True, False))
    pl.when(jnp.logical_and(j > 0, j < last))(lambda: step(False, False))
    pl.when(j == last)(lambda: step(False, True))


def _mlp(x, gain_in, gain_out, w1, w2, layer, *, tm=1024, tf=MLP_HIDDEN_TILE, sub_rows=512):
    t, d = x.shape
    f = w1.shape[2]
    tm = min(tm, t)
    sub_rows = min(sub_rows, tm)
    assert t % tm == 0 and tm % sub_rows == 0 and f % tf == 0 and f // tf >= 2
    return pl.pallas_call(
        functools.partial(_mlp_kernel, sub_rows=sub_rows),
        grid=(t // tm, f // tf),
        in_specs=[
            pl.BlockSpec((tm, d), lambda i, j: (i, 0)),
            pl.BlockSpec((1, d), lambda i, j: (0, 0)),
            pl.BlockSpec((1, d), lambda i, j: (0, 0)),
            pl.BlockSpec((None, d, tf), lambda i, j: (layer, 0, j)),
            pl.BlockSpec((None, tf, d), lambda i, j: (layer, j, 0)),
        ],
        out_specs=pl.BlockSpec((tm, d), lambda i, j: (i, 0)),
        out_shape=jax.ShapeDtypeStruct((t, d), F32),
        scratch_shapes=[pltpu.VMEM((tm, d), BF16)],
        compiler_params=_params("parallel", "arbitrary"),
        name="mlp",
    )(x, gain_in.reshape(1, d), gain_out.reshape(1, d), w1, w2)


def _sequences_per_step(b, l):
    nb = max(1, min(b, MIXER_STEP_TOKENS // l))
    while b % nb:
        nb -= 1
    return nb


def _retention_kernel(dec_ref, q_ref, k_ref, v_ref, g_ref, o_ref, kvb_ref, sf_ref, kt_ref, *, n_chunks):
    c = RET_CHUNK
    unroll = min(RET_UNROLL, n_chunks)
    log_gamma = jnp.log1p(-jnp.exp2(-dec_ref[...]))
    lgf, lgb = log_gamma[0:1, :], log_gamma[1:2, :]
    lane = lax.broadcasted_iota(jnp.int32, (1, c), 1).astype(F32)
    k_decay_f_row = jnp.exp((c - 1.0 - lane) * lgf).astype(BF16)
    k_decay_b_row = jnp.exp(lane * lgb).astype(BF16)
    chunk_decay_f = jnp.exp(c * lgf)
    chunk_decay_b = jnp.exp(c * lgb)
    row = lax.broadcasted_iota(jnp.int32, (c, c), 0).astype(F32)
    col = lax.broadcasted_iota(jnp.int32, (c, c), 1).astype(F32)
    diff = row - col
    decay_intra = jnp.exp(jnp.where(diff >= 0, diff * lgf, -diff * lgb))
    pos = lax.broadcasted_iota(jnp.int32, (c, HEAD_DIM), 0).astype(F32)
    q_decay_f = jnp.exp((pos + 1.0) * lgf).astype(BF16)
    q_decay_b = jnp.exp((c - pos) * lgb).astype(BF16)
    zero_state = jnp.zeros((HEAD_DIM, HEAD_DIM), F32)

    def one_sequence(q, k, v, g, o, base):
        def forward_group(i, sf):
            chunks = [i * unroll + u for u in range(unroll)]
            offs = [pl.multiple_of(n * c, c) for n in chunks]
            lhs = []
            for n, off in zip(chunks, offs):
                kr_t = k[pl.ds(off, c), :].T
                kt_ref[base + n] = kr_t
                lhs.append(jnp.concatenate([kr_t * k_decay_f_row, kr_t * k_decay_b_row], axis=0))
            kvs = [jnp.dot(lhs[u], v[pl.ds(off, c), :], preferred_element_type=F32)
                   for u, off in enumerate(offs)]
            for n, kv in zip(chunks, kvs):
                sf_ref[base + n] = sf.astype(BF16)
                sf = chunk_decay_f * sf + kv[0:HEAD_DIM, :]
                kvb_ref[base + n] = kv[HEAD_DIM:2 * HEAD_DIM, :]
            return sf

        lax.fori_loop(0, n_chunks // unroll, forward_group, zero_state)

        def backward_group(t, sb):
            i = n_chunks // unroll - 1 - t
            chunks = [i * unroll + u for u in range(unroll)]
            offs = [pl.multiple_of(n * c, c) for n in chunks]
            scores = [jnp.dot(q[pl.ds(off, c), :], kt_ref[base + n], preferred_element_type=F32)
                      for n, off in zip(chunks, offs)]
            states = [None] * unroll
            for u in reversed(range(unroll)):
                states[u] = sb.astype(BF16)
                sb = chunk_decay_b * sb + kvb_ref[base + chunks[u]]
            ys = []
            for n, off, s, sb_n in zip(chunks, offs, scores, states):
                qr = q[pl.ds(off, c), :]
                lhs = jnp.concatenate([(s * decay_intra).astype(BF16), qr * q_decay_f, qr * q_decay_b], axis=1)
                rhs = jnp.concatenate([v[pl.ds(off, c), :], sf_ref[base + n], sb_n], axis=0)
                ys.append(jnp.dot(lhs, rhs, preferred_element_type=F32))
            for off, y in zip(offs, ys):
                y = y * lax.rsqrt(jnp.mean(y * y, axis=-1, keepdims=True) + NORM_EPS)
                o[pl.ds(off, c), :] = (g[pl.ds(off, c), :].astype(F32) * y).astype(o.dtype)
            return sb

        lax.fori_loop(0, n_chunks // unroll, backward_group, zero_state)

    for s in range(q_ref.shape[0]):
        one_sequence(q_ref.at[s], k_ref.at[s], v_ref.at[s], g_ref.at[s], o_ref.at[s], s * n_chunks)


def _rotary_tables(l):
    half = HEAD_DIM // 2
    inv_freq = ROPE_BASE ** (-jnp.arange(half, dtype=F32) / half)
    ang = jnp.arange(l, dtype=F32)[:, None] * inv_freq[None, :]
    cos, sin = jnp.cos(ang), jnp.sin(ang)
    return jnp.concatenate([cos, cos], axis=-1), jnp.concatenate([-sin, sin], axis=-1)


def _retention(proj, decay_exp, tok_heads):
    b, l, _ = proj.shape
    assert l % RET_CHUNK == 0 and (l // RET_CHUNK) % min(RET_UNROLL, l // RET_CHUNK) == 0
    dec = jnp.broadcast_to(decay_exp.astype(F32).T[:, :, None], (tok_heads, 2, HEAD_DIM))
    nb = _sequences_per_step(b, l)
    head_spec = lambda part: pl.BlockSpec((nb, l, HEAD_DIM), lambda bi, h: (bi, 0, part * tok_heads + h))
    n_chunks = l // RET_CHUNK
    return pl.pallas_call(
        functools.partial(_retention_kernel, n_chunks=n_chunks),
        grid=(b // nb, tok_heads),
        in_specs=[pl.BlockSpec((None, 2, HEAD_DIM), lambda bi, h: (h, 0, 0)),
                  head_spec(0), head_spec(1), head_spec(2), head_spec(3)],
        out_specs=pl.BlockSpec((nb, l, HEAD_DIM), lambda bi, h: (bi, 0, h)),
        out_shape=jax.ShapeDtypeStruct((b, l, tok_heads * HEAD_DIM), BF16),
        scratch_shapes=[pltpu.VMEM((nb * n_chunks, HEAD_DIM, HEAD_DIM), F32),
                        pltpu.VMEM((nb * n_chunks, HEAD_DIM, HEAD_DIM), BF16),
                        pltpu.VMEM((nb * n_chunks, HEAD_DIM, RET_CHUNK), BF16)],
        compiler_params=_params("parallel", "parallel"),
        name="retention",
    )(dec, proj, proj, proj, proj)


def _na_kernel(bias_ref, q_ref, k_ref, v_ref, o_ref, *, rows):
    win = NA_KH * GRID_W
    unroll = min(NA_UNROLL, rows)

    def one_sequence(q, k, v, o):
        def rows_step(i, carry):
            qrows = [i * unroll + u for u in range(unroll)]
            starts = [jnp.clip(r - NA_KH // 2, 0, rows - NA_KH) for r in qrows]
            q_offs = [pl.multiple_of(r * GRID_W, GRID_W) for r in qrows]
            k_offs = [pl.multiple_of(rs * GRID_W, GRID_W) for rs in starts]
            scores = [lax.dot_general(q[pl.ds(q_off, GRID_W), :], k[pl.ds(k_off, win), :], NT_DIMS,
                                      preferred_element_type=F32)
                      for q_off, k_off in zip(q_offs, k_offs)]
            probs, denoms = [], []
            for r, rs, s in zip(qrows, starts, scores):
                first = rs - r + NA_KH - 1
                s = s + jnp.concatenate([bias_ref[first + 2 * j] for j in range(NA_KH // 2)], axis=1)
                p = jnp.exp2(s - jnp.max(s, axis=-1, keepdims=True))
                denoms.append(jnp.sum(p, axis=-1, keepdims=True))
                probs.append(p.astype(BF16))
            outs = [jnp.dot(p, v[pl.ds(k_off, win), :], preferred_element_type=F32)
                    for p, k_off in zip(probs, k_offs)]
            for q_off, out, denom in zip(q_offs, outs, denoms):
                o[pl.ds(q_off, GRID_W), :] = (out / denom).astype(o.dtype)
            return carry

        lax.fori_loop(0, rows // unroll, rows_step, 0)

    for s in range(q_ref.shape[0]):
        one_sequence(q_ref.at[s], k_ref.at[s], v_ref.at[s], o_ref.at[s])


def _na_bias_table(rpb):
    c = np.arange(GRID_W)
    win_start = np.clip(c - NA_KW // 2, 0, GRID_W - NA_KW)
    valid = (c[None, :] >= win_start[:, None]) & (c[None, :] < win_start[:, None] + NA_KW)
    pad = GRID_W - NA_KW
    padded = jnp.pad(rpb.astype(F32), ((0, 0), (0, 0), (pad, pad)))
    toeplitz = jnp.stack([padded[:, :, GRID_W - 1 - ci:2 * GRID_W - 1 - ci] for ci in range(GRID_W)], axis=2)
    toeplitz = jnp.where(valid[None, None], toeplitz * LOG2_E, -jnp.inf)
    return jnp.concatenate([toeplitz[:, :-1], toeplitz[:, 1:]], axis=-1)


def _neighbourhood(proj, rpb, tok_heads):
    b, l, _ = proj.shape
    rows = l // GRID_W
    assert l % GRID_W == 0 and rows >= NA_KH and rows % min(NA_UNROLL, rows) == 0
    bias = _na_bias_table(rpb)
    nb = _sequences_per_step(b, l)
    head_spec = lambda part: pl.BlockSpec((nb, l, HEAD_DIM), lambda bi, h: (bi, 0, part * tok_heads + h))
    return pl.pallas_call(
        functools.partial(_na_kernel, rows=rows),
        grid=(b // nb, tok_heads),
        in_specs=[pl.BlockSpec((None, 2 * NA_KH - 2, GRID_W, 2 * GRID_W), lambda bi, h: (h, 0, 0, 0)),
                  head_spec(0), head_spec(1), head_spec(2)],
        out_specs=pl.BlockSpec((nb, l, HEAD_DIM), lambda bi, h: (bi, 0, h)),
        out_shape=jax.ShapeDtypeStruct((b, l, tok_heads * HEAD_DIM), BF16),
        compiler_params=_params("parallel", "parallel"),
        name="neighbourhood",
    )(bias, proj, proj, proj)


def _trunk(x, mem, norm_gain, mem_norm_gain, w_mem_kv, w_out, w_mlp_in, w_mlp_out, w_in_ret, ret_decay,
           w_in_na, na_rpb):
    b, l, d = x.shape
    n_mem = mem.shape[1]
    tok_width = w_out.shape[1] - MEM_WIDTH
    tok_heads = tok_width // HEAD_DIM
    xf = x.reshape(b * l, d)
    memf = mem.reshape(b * n_mem, d)
    for i in range(norm_gain.shape[0]):
        g = norm_gain[i]
        mem_kv = _norm_matmul(memf, mem_norm_gain[i], w_mem_kv, i).reshape(b, n_mem, 2 * MEM_WIDTH)
        j = i // N_MIXERS
        if i % N_MIXERS == 0:
            proj = _norm_matmul(xf, g[0], w_in_ret, j, columns="retention", tok_width=tok_width,
                                seq=l).reshape(b, l, -1)
            tok = _retention(proj, ret_decay[j], tok_heads)
        else:
            proj = _norm_matmul(xf, g[0], w_in_na, j, columns="neighbourhood", tok_width=tok_width,
                                seq=l).reshape(b, l, -1)
            tok = _neighbourhood(proj, na_rpb[j], tok_heads)
        x1 = _out_proj(tok.reshape(b * l, tok_width), proj.reshape(b * l, -1), mem_kv, xf, g[1], w_out, i)
        xf = _mlp(x1, g[2], g[3], w_mlp_in, w_mlp_out, i)
    return xf.reshape(b, l, d)


def kernel(x_prompt, x_sample, mem_prompt, mem_sample, norm_gain, mem_norm_gain, w_mem_kv, w_out, w_mlp_in,
           w_mlp_out, w_in_ret, ret_decay, w_in_na, na_rpb):
    weights = [w.astype(BF16) for w in (w_mem_kv, w_out, w_mlp_in, w_mlp_out, w_in_ret)]
    w_mem_kv, w_out, w_mlp_in, w_mlp_out, w_in_ret = weights
    w_in_na = w_in_na.astype(BF16)
    run = functools.partial(_trunk, norm_gain=norm_gain, mem_norm_gain=mem_norm_gain, w_mem_kv=w_mem_kv,
                            w_out=w_out, w_mlp_in=w_mlp_in, w_mlp_out=w_mlp_out, w_in_ret=w_in_ret,
                            ret_decay=ret_decay, w_in_na=w_in_na, na_rpb=na_rpb)
    return (run(x_prompt, mem_prompt), run(x_sample, mem_sample))
```

```python
import functools

import jax
import jax.numpy as jnp
import numpy as np
from jax import lax
from jax.experimental import pallas as pl
from jax.experimental.pallas import tpu as pltpu

HEAD_DIM = 128
MEM_HEADS = 4
MEM_WIDTH = MEM_HEADS * HEAD_DIM
N_MIXERS = 2
RET_CHUNK = 128
ROPE_BASE = 10000.0
GRID_W = 64
NA_KH = 8
NA_KW = 16
NORM_EPS = 1e-6
ATTN_SCALE = HEAD_DIM ** -0.5
LOG2_E = 1.4426950408889634
MLP_HIDDEN_TILE = 1024
MIXER_STEP_TOKENS = 8192
RET_UNROLL = 64
NA_UNROLL = 64

V7X_VMEM_BYTES = 64 * 1024 * 1024
VMEM_LIMIT_BYTES = V7X_VMEM_BYTES - 4 * 1024 * 1024

F32 = jnp.float32
BF16 = jnp.bfloat16
NT_DIMS = (((1,), (1,)), ((), ()))


def _params(*semantics):
    return pltpu.CompilerParams(dimension_semantics=semantics, vmem_limit_bytes=VMEM_LIMIT_BYTES)


def _rms(x, gain):
    return x * lax.rsqrt(jnp.mean(x * x, axis=-1, keepdims=True) + NORM_EPS) * gain


def _rotate(x, cos2, sin2):
    return x * cos2 + pltpu.roll(x, HEAD_DIM // 2, axis=1) * sin2


def _retention_columns(res, cos2, sin2, tok_width):
    heads = tok_width // HEAD_DIM
    blocks = []
    for c in range(res.shape[1] // HEAD_DIM):
        blk = res[:, c * HEAD_DIM:(c + 1) * HEAD_DIM]
        part = c // heads
        if part == 0:
            blk = _rotate(blk, cos2, sin2)
        elif part == 1:
            blk = _rotate(blk, cos2, sin2) * ATTN_SCALE
        elif part == 3:
            blk = blk * (1.0 / (1.0 + jnp.exp(-blk)))
        blocks.append(blk.astype(BF16))
    return jnp.concatenate(blocks, axis=1)


def _neighbourhood_columns(res, tok_width):
    q = res[:, :tok_width] * (ATTN_SCALE * LOG2_E)
    return jnp.concatenate([q.astype(BF16), res[:, tok_width:].astype(BF16)], axis=1)


def _norm_matmul_kernel(x_ref, g_ref, w_ref, *rest, sub_rows, columns, tok_width):
    o_ref = rest[-1]
    gain = g_ref[...]
    for r in range(x_ref.shape[0] // sub_rows):
        rows = slice(r * sub_rows, (r + 1) * sub_rows)
        h = _rms(x_ref[rows, :], gain).astype(BF16)
        res = jnp.dot(h, w_ref[...], preferred_element_type=F32)
        if columns == "retention":
            cos_ref, sin_ref = rest[0], rest[1]
            res = _retention_columns(res, cos_ref[rows, :], sin_ref[rows, :], tok_width)
        elif columns == "neighbourhood":
            res = _neighbourhood_columns(res, tok_width)
        o_ref[rows, :] = res.astype(o_ref.dtype)


def _norm_matmul(x, gain, w, layer, *, columns=None, tok_width=None, seq=None, tm=512, sub_rows=256):
    t, d = x.shape
    n = w.shape[2]
    tm = min(tm, t if seq is None else seq)
    sub_rows = min(sub_rows, tm)
    assert t % tm == 0 and tm % sub_rows == 0
    operands = [x, gain.reshape(1, d), w]
    in_specs = [
        pl.BlockSpec((tm, d), lambda i: (i, 0)),
        pl.BlockSpec((1, d), lambda i: (0, 0)),
        pl.BlockSpec((None, d, n), lambda i: (layer, 0, 0), pipeline_mode=pl.Buffered(1)),
    ]
    if columns == "retention":
        assert seq % tm == 0
        tiles_per_seq = seq // tm
        operands += list(_rotary_tables(seq))
        in_specs += [pl.BlockSpec((tm, HEAD_DIM), lambda i: (i % tiles_per_seq, 0)),
                     pl.BlockSpec((tm, HEAD_DIM), lambda i: (i % tiles_per_seq, 0))]
    return pl.pallas_call(
        functools.partial(_norm_matmul_kernel, sub_rows=sub_rows, columns=columns, tok_width=tok_width),
        grid=(t // tm,),
        in_specs=in_specs,
        out_specs=pl.BlockSpec((tm, n), lambda i: (i, 0)),
        out_shape=jax.ShapeDtypeStruct((t, n), BF16),
        compiler_params=_params("parallel"),
        name="norm_matmul",
    )(*operands)


def _mem_scores(q, k_ref):
    return [lax.dot_general(q[:, h * HEAD_DIM:(h + 1) * HEAD_DIM], k_ref[:, h * HEAD_DIM:(h + 1) * HEAD_DIM],
                            NT_DIMS, preferred_element_type=F32) for h in range(MEM_HEADS)]


def _mem_values(scores, v_ref):
    probs, denoms = [], []
    for s in scores:
        s = s * ATTN_SCALE
        p = jnp.exp(s - jnp.max(s, axis=-1, keepdims=True))
        denoms.append(jnp.sum(p, axis=-1, keepdims=True))
        probs.append(p.astype(BF16))
    outs = [jnp.dot(p, v_ref[:, h * HEAD_DIM:(h + 1) * HEAD_DIM], preferred_element_type=F32)
            for h, p in enumerate(probs)]
    return jnp.concatenate([(o / dn).astype(BF16) for o, dn in zip(outs, denoms)], axis=1)


def _out_proj_kernel(tok_ref, qm_ref, k_ref, v_ref, x_ref, g_ref, wt_ref, wm_ref, o_ref, *, sub_rows):
    gain = g_ref[...]
    n_sub = x_ref.shape[0] // sub_rows
    rows = [slice(r * sub_rows, (r + 1) * sub_rows) for r in range(n_sub)]
    mem = _mem_values(_mem_scores(qm_ref[rows[0], :], k_ref), v_ref)
    for r in range(n_sub):
        if r + 1 < n_sub:
            next_scores = _mem_scores(qm_ref[rows[r + 1], :], k_ref)
        mixed = (jnp.dot(tok_ref[rows[r], :], wt_ref[...], preferred_element_type=F32)
                 + jnp.dot(mem, wm_ref[...], preferred_element_type=F32))
        if r + 1 < n_sub:
            mem = _mem_values(next_scores, v_ref)
        o_ref[rows[r], :] = x_ref[rows[r], :] + _rms(mixed, gain)


def _out_proj(tok, proj, mem_kv, x, gain, w_out, layer, *, tm=1024, sub_rows=256):
    t, d = x.shape
    b, m, _ = mem_kv.shape
    seq = t // b
    tm = min(tm, seq)
    sub_rows = min(sub_rows, tm)
    tw, n = tok.shape[1], proj.shape[1]
    assert seq % tm == 0 and tm % sub_rows == 0 and tw % MEM_WIDTH == 0 and n % MEM_WIDTH == 0
    assert w_out.shape[1] == tw + MEM_WIDTH
    tiles_per_seq = seq // tm
    q_block = n // MEM_WIDTH - 1
    return pl.pallas_call(
        functools.partial(_out_proj_kernel, sub_rows=sub_rows),
        grid=(t // tm,),
        in_specs=[
            pl.BlockSpec((tm, tw), lambda i: (i, 0)),
            pl.BlockSpec((tm, MEM_WIDTH), lambda i: (i, q_block)),
            pl.BlockSpec((None, m, MEM_WIDTH), lambda i: (i // tiles_per_seq, 0, 0)),
            pl.BlockSpec((None, m, MEM_WIDTH), lambda i: (i // tiles_per_seq, 0, 1)),
            pl.BlockSpec((tm, d), lambda i: (i, 0)),
            pl.BlockSpec((1, d), lambda i: (0, 0)),
            pl.BlockSpec((None, tw, d), lambda i: (layer, 0, 0), pipeline_mode=pl.Buffered(1)),
            pl.BlockSpec((None, MEM_WIDTH, d), lambda i: (layer, tw // MEM_WIDTH, 0),
                         pipeline_mode=pl.Buffered(1)),
        ],
        out_specs=pl.BlockSpec((tm, d), lambda i: (i, 0)),
        out_shape=jax.ShapeDtypeStruct((t, d), F32),
        compiler_params=_params("parallel"),
        name="out_proj",
    )(tok, proj, mem_kv, mem_kv, x, gain.reshape(1, d), w_out, w_out)


def _mlp_kernel(x_ref, gi_ref, go_ref, w1_hbm, w2_hbm, o_ref, h_ref, *, sub_rows, layer, tf):
    d = x_ref.shape[1]
    n_sub = x_ref.shape[0] // sub_rows
    rows = [slice(r * sub_rows, (r + 1) * sub_rows) for r in range(n_sub)]
    for r in rows:
        h_ref[r, :] = _rms(x_ref[r, :], gi_ref[...]).astype(BF16)
        o_ref[r, :] = jnp.zeros((sub_rows, d), F32)

    def hidden_tile(w1_ref, w2_ref):
        hidden = [jnp.square(jnp.maximum(jnp.dot(h_ref[r, :], w1_ref[...], preferred_element_type=F32), 0.0))
                  .astype(BF16) for r in rows]
        for r, u in zip(rows, hidden):
            o_ref[r, :] += jnp.dot(u, w2_ref[...], preferred_element_type=F32)

    w1, w2 = w1_hbm.at[layer], w2_hbm.at[layer]
    pltpu.emit_pipeline(
        hidden_tile,
        grid=(w1.shape[1] // tf,),
        in_specs=[pl.BlockSpec((d, tf), lambda j: (0, j)), pl.BlockSpec((tf, d), lambda j: (j, 0))],
    )(w1, w2)

    for r in rows:
        o_ref[r, :] = x_ref[r, :] + _rms(o_ref[r, :], go_ref[...])


def _mlp(x, gain_in, gain_out, w1, w2, layer, *, tm=1024, tf=MLP_HIDDEN_TILE, sub_rows=512):
    t, d = x.shape
    f = w1.shape[2]
    tm = min(tm, t)
    sub_rows = min(sub_rows, tm)
    assert t % tm == 0 and tm % sub_rows == 0 and f % tf == 0
    return pl.pallas_call(
        functools.partial(_mlp_kernel, sub_rows=sub_rows, layer=layer, tf=tf),
        grid=(t // tm,),
        in_specs=[
            pl.BlockSpec((tm, d), lambda i: (i, 0)),
            pl.BlockSpec((1, d), lambda i: (0, 0)),
            pl.BlockSpec((1, d), lambda i: (0, 0)),
            pl.BlockSpec(memory_space=pl.ANY),
            pl.BlockSpec(memory_space=pl.ANY),
        ],
        out_specs=pl.BlockSpec((tm, d), lambda i: (i, 0)),
        out_shape=jax.ShapeDtypeStruct((t, d), F32),
        scratch_shapes=[pltpu.VMEM((tm, d), BF16)],
        compiler_params=_params("arbitrary"),
        name="mlp",
    )(x, gain_in.reshape(1, d), gain_out.reshape(1, d), w1, w2)


def _sequences_per_step(b, l):
    nb = max(1, min(b, MIXER_STEP_TOKENS // l))
    while b % nb:
        nb -= 1
    return nb


def _retention_kernel(dec_ref, q_ref, k_ref, v_ref, g_ref, o_ref, kvb_ref, sf_ref, kt_ref, *, n_chunks):
    c = RET_CHUNK
    unroll = min(RET_UNROLL, n_chunks)
    log_gamma = jnp.log1p(-jnp.exp2(-dec_ref[...]))
    lgf, lgb = log_gamma[0:1, :], log_gamma[1:2, :]
    lane = lax.broadcasted_iota(jnp.int32, (1, c), 1).astype(F32)
    k_decay_f_row = jnp.exp((c - 1.0 - lane) * lgf).astype(BF16)
    k_decay_b_row = jnp.exp(lane * lgb).astype(BF16)
    chunk_decay_f = jnp.exp(c * lgf)
    chunk_decay_b = jnp.exp(c * lgb)
    row = lax.broadcasted_iota(jnp.int32, (c, c), 0).astype(F32)
    col = lax.broadcasted_iota(jnp.int32, (c, c), 1).astype(F32)
    diff = row - col
    decay_intra = jnp.exp(jnp.where(diff >= 0, diff * lgf, -diff * lgb))
    pos = lax.broadcasted_iota(jnp.int32, (c, HEAD_DIM), 0).astype(F32)
    q_decay_f = jnp.exp((pos + 1.0) * lgf).astype(BF16)
    q_decay_b = jnp.exp((c - pos) * lgb).astype(BF16)
    zero_state = jnp.zeros((HEAD_DIM, HEAD_DIM), F32)

    def one_sequence(q, k, v, g, o, base):
        def forward_group(i, sf):
            chunks = [i * unroll + u for u in range(unroll)]
            offs = [pl.multiple_of(n * c, c) for n in chunks]
            lhs = []
            for n, off in zip(chunks, offs):
                kr_t = k[pl.ds(off, c), :].T
                kt_ref[base + n] = kr_t
                lhs.append(jnp.concatenate([kr_t * k_decay_f_row, kr_t * k_decay_b_row], axis=0))
            kvs = [jnp.dot(lhs[u], v[pl.ds(off, c), :], preferred_element_type=F32)
                   for u, off in enumerate(offs)]
            for n, kv in zip(chunks, kvs):
                sf_ref[base + n] = sf.astype(BF16)
                sf = chunk_decay_f * sf + kv[0:HEAD_DIM, :]
                kvb_ref[base + n] = kv[HEAD_DIM:2 * HEAD_DIM, :]
            return sf

        lax.fori_loop(0, n_chunks // unroll, forward_group, zero_state)

        def backward_group(t, sb):
            i = n_chunks // unroll - 1 - t
            chunks = [i * unroll + u for u in range(unroll)]
            offs = [pl.multiple_of(n * c, c) for n in chunks]
            scores = [jnp.dot(q[pl.ds(off, c), :], kt_ref[base + n], preferred_element_type=F32)
                      for n, off in zip(chunks, offs)]
            states = [None] * unroll
            for u in reversed(range(unroll)):
                states[u] = sb.astype(BF16)
                sb = chunk_decay_b * sb + kvb_ref[base + chunks[u]]
            ys = []
            for n, off, s, sb_n in zip(chunks, offs, scores, states):
                qr = q[pl.ds(off, c), :]
                lhs = jnp.concatenate([(s * decay_intra).astype(BF16), qr * q_decay_f, qr * q_decay_b], axis=1)
                rhs = jnp.concatenate([v[pl.ds(off, c), :], sf_ref[base + n], sb_n], axis=0)
                ys.append(jnp.dot(lhs, rhs, preferred_element_type=F32))
            for off, y in zip(offs, ys):
                y = y * lax.rsqrt(jnp.mean(y * y, axis=-1, keepdims=True) + NORM_EPS)
                o[pl.ds(off, c), :] = (g[pl.ds(off, c), :].astype(F32) * y).astype(o.dtype)
            return sb

        lax.fori_loop(0, n_chunks // unroll, backward_group, zero_state)

    for s in range(q_ref.shape[0]):
        one_sequence(q_ref.at[s], k_ref.at[s], v_ref.at[s], g_ref.at[s], o_ref.at[s], s * n_chunks)


def _rotary_tables(l):
    half = HEAD_DIM // 2
    inv_freq = ROPE_BASE ** (-jnp.arange(half, dtype=F32) / half)
    ang = jnp.arange(l, dtype=F32)[:, None] * inv_freq[None, :]
    cos, sin = jnp.cos(ang), jnp.sin(ang)
    return jnp.concatenate([cos, cos], axis=-1), jnp.concatenate([-sin, sin], axis=-1)


def _retention(proj, decay_exp, tok_heads):
    b, l, _ = proj.shape
    assert l % RET_CHUNK == 0 and (l // RET_CHUNK) % min(RET_UNROLL, l // RET_CHUNK) == 0
    dec = jnp.broadcast_to(decay_exp.astype(F32).T[:, :, None], (tok_heads, 2, HEAD_DIM))
    nb = _sequences_per_step(b, l)
    head_spec = lambda part: pl.BlockSpec((nb, l, HEAD_DIM), lambda bi, h: (bi, 0, part * tok_heads + h))
    n_chunks = l // RET_CHUNK
    return pl.pallas_call(
        functools.partial(_retention_kernel, n_chunks=n_chunks),
        grid=(b // nb, tok_heads),
        in_specs=[pl.BlockSpec((None, 2, HEAD_DIM), lambda bi, h: (h, 0, 0)),
                  head_spec(0), head_spec(1), head_spec(2), head_spec(3)],
        out_specs=pl.BlockSpec((nb, l, HEAD_DIM), lambda bi, h: (bi, 0, h)),
        out_shape=jax.ShapeDtypeStruct((b, l, tok_heads * HEAD_DIM), BF16),
        scratch_shapes=[pltpu.VMEM((nb * n_chunks, HEAD_DIM, HEAD_DIM), F32),
                        pltpu.VMEM((nb * n_chunks, HEAD_DIM, HEAD_DIM), BF16),
                        pltpu.VMEM((nb * n_chunks, HEAD_DIM, RET_CHUNK), BF16)],
        compiler_params=_params("parallel", "parallel"),
        name="retention",
    )(dec, proj, proj, proj, proj)


def _na_kernel(bias_ref, q_ref, k_ref, v_ref, o_ref, *, rows):
    win = NA_KH * GRID_W
    unroll = min(NA_UNROLL, rows)

    def one_sequence(q, k, v, o):
        def rows_step(i, carry):
            qrows = [i * unroll + u for u in range(unroll)]
            starts = [jnp.clip(r - NA_KH // 2, 0, rows - NA_KH) for r in qrows]
            q_offs = [pl.multiple_of(r * GRID_W, GRID_W) for r in qrows]
            k_offs = [pl.multiple_of(rs * GRID_W, GRID_W) for rs in starts]
            scores = [lax.dot_general(q[pl.ds(q_off, GRID_W), :], k[pl.ds(k_off, win), :], NT_DIMS,
                                      preferred_element_type=F32)
                      for q_off, k_off in zip(q_offs, k_offs)]
            probs, denoms = [], []
            for r, rs, s in zip(qrows, starts, scores):
                first = rs - r + NA_KH - 1
                s = s + jnp.concatenate([bias_ref[first + 2 * j] for j in range(NA_KH // 2)], axis=1)
                p = jnp.exp2(s - jnp.max(s, axis=-1, keepdims=True))
                denoms.append(jnp.sum(p, axis=-1, keepdims=True))
                probs.append(p.astype(BF16))
            outs = [jnp.dot(p, v[pl.ds(k_off, win), :], preferred_element_type=F32)
                    for p, k_off in zip(probs, k_offs)]
            for q_off, out, denom in zip(q_offs, outs, denoms):
                o[pl.ds(q_off, GRID_W), :] = (out / denom).astype(o.dtype)
            return carry

        lax.fori_loop(0, rows // unroll, rows_step, 0)

    for s in range(q_ref.shape[0]):
        one_sequence(q_ref.at[s], k_ref.at[s], v_ref.at[s], o_ref.at[s])


def _na_bias_table(rpb):
    c = np.arange(GRID_W)
    win_start = np.clip(c - NA_KW // 2, 0, GRID_W - NA_KW)
    valid = (c[None, :] >= win_start[:, None]) & (c[None, :] < win_start[:, None] + NA_KW)
    pad = GRID_W - NA_KW
    padded = jnp.pad(rpb.astype(F32), ((0, 0), (0, 0), (pad, pad)))
    toeplitz = jnp.stack([padded[:, :, GRID_W - 1 - ci:2 * GRID_W - 1 - ci] for ci in range(GRID_W)], axis=2)
    toeplitz = jnp.where(valid[None, None], toeplitz * LOG2_E, -jnp.inf)
    return jnp.concatenate([toeplitz[:, :-1], toeplitz[:, 1:]], axis=-1)


def _neighbourhood(proj, rpb, tok_heads):
    b, l, _ = proj.shape
    rows = l // GRID_W
    assert l % GRID_W == 0 and rows >= NA_KH and rows % min(NA_UNROLL, rows) == 0
    bias = _na_bias_table(rpb)
    nb = _sequences_per_step(b, l)
    head_spec = lambda part: pl.BlockSpec((nb, l, HEAD_DIM), lambda bi, h: (bi, 0, part * tok_heads + h))
    return pl.pallas_call(
        functools.partial(_na_kernel, rows=rows),
        grid=(b // nb, tok_heads),
        in_specs=[pl.BlockSpec((None, 2 * NA_KH - 2, GRID_W, 2 * GRID_W), lambda bi, h: (h, 0, 0, 0)),
                  head_spec(0), head_spec(1), head_spec(2)],
        out_specs=pl.BlockSpec((nb, l, HEAD_DIM), lambda bi, h: (bi, 0, h)),
        out_shape=jax.ShapeDtypeStruct((b, l, tok_heads * HEAD_DIM), BF16),
        compiler_params=_params("parallel", "parallel"),
        name="neighbourhood",
    )(bias, proj, proj, proj)


def _trunk(x, mem, norm_gain, mem_norm_gain, w_mem_kv, w_out, w_mlp_in, w_mlp_out, w_in_ret, ret_decay,
           w_in_na, na_rpb):
    b, l, d = x.shape
    n_mem = mem.shape[1]
    tok_width = w_out.shape[1] - MEM_WIDTH
    tok_heads = tok_width // HEAD_DIM
    xf = x.reshape(b * l, d)
    memf = mem.reshape(b * n_mem, d)
    for i in range(norm_gain.shape[0]):
        g = norm_gain[i]
        mem_kv = _norm_matmul(memf, mem_norm_gain[i], w_mem_kv, i).reshape(b, n_mem, 2 * MEM_WIDTH)
        j = i // N_MIXERS
        if i % N_MIXERS == 0:
            proj = _norm_matmul(xf, g[0], w_in_ret, j, columns="retention", tok_width=tok_width,
                                seq=l).reshape(b, l, -1)
            tok = _retention(proj, ret_decay[j], tok_heads)
        else:
            proj = _norm_matmul(xf, g[0], w_in_na, j, columns="neighbourhood", tok_width=tok_width,
                                seq=l).reshape(b, l, -1)
            tok = _neighbourhood(proj, na_rpb[j], tok_heads)
        x1 = _out_proj(tok.reshape(b * l, tok_width), proj.reshape(b * l, -1), mem_kv, xf, g[1], w_out, i)
        xf = _mlp(x1, g[2], g[3], w_mlp_in, w_mlp_out, i)
    return xf.reshape(b, l, d)


def kernel(x_prompt, x_sample, mem_prompt, mem_sample, norm_gain, mem_norm_gain, w_mem_kv, w_out, w_mlp_in,
           w_mlp_out, w_in_ret, ret_decay, w_in_na, na_rpb):
    weights = [w.astype(BF16) for w in (w_mem_kv, w_out, w_mlp_in, w_mlp_out, w_in_ret)]
    w_mem_kv, w_out, w_mlp_in, w_mlp_out, w_in_ret = weights
    w_in_na = w_in_na.astype(BF16)
    run = functools.partial(_trunk, norm_gain=norm_gain, mem_norm_gain=mem_norm_gain, w_mem_kv=w_mem_kv,
                            w_out=w_out, w_mlp_in=w_mlp_in, w_mlp_out=w_mlp_out, w_in_ret=w_in_ret,
                            ret_decay=ret_decay, w_in_na=w_in_na, na_rpb=na_rpb)
    return (run(x_prompt, mem_prompt), run(x_sample, mem_sample))
```
